```python
import math
import jax, jax.numpy as jnp
from jax import lax
import numpy as np

D_MODEL = 2048
BATCH = 1
SEQ = 16384
DEPTH = 1

MIX_WIDTH = D_MODEL
A_HEAD_DIM = 128
A_WIDTH = MIX_WIDTH // 2
A_HEADS = A_WIDTH // A_HEAD_DIM
A_PATTERNS = ((128, 1), (512, 4), (2048, 16))
B_HEAD_DIM = 64
B_WIDTH = MIX_WIDTH - A_WIDTH
B_HEADS = B_WIDTH // B_HEAD_DIM
B_KV_HEADS = 2
B_GROUP = B_HEADS // B_KV_HEADS
B_KV_WIDTH = B_KV_HEADS * B_HEAD_DIM
B_WINDOW = 128
IN_WIDTH = 3 * A_WIDTH + B_WIDTH + 2 * B_KV_WIDTH
FF_MULTIPLE = 256
D_FF = ((8 * D_MODEL + 3 * FF_MULTIPLE - 1) // (3 * FF_MULTIPLE)) * FF_MULTIPLE
BLOCK = 128
ROPE_THETA = 10000.0
ALPHA = (2 * DEPTH) ** 0.25
BETA = (8 * DEPTH) ** -0.25
LN_EPS = 1e-5
RMS_EPS = 1e-6

kernel_name = "hymba_dilated_sinkswa_deepnorm_swiglu"


def layer_norm(x, g, b):
    xf = x.astype(jnp.float32)
    mu = xf.mean(-1, keepdims=True)
    var = jnp.square(xf - mu).mean(-1, keepdims=True)
    return ((xf - mu) * lax.rsqrt(var + LN_EPS) * g + b).astype(x.dtype)


def rms_norm(x, g):
    xf = x.astype(jnp.float32)
    return xf * lax.rsqrt(jnp.square(xf).mean(-1, keepdims=True) + RMS_EPS) * g


def rope(x, pos):
    half = x.shape[-1] // 2
    inv_freq = ROPE_THETA ** (-jnp.arange(half, dtype=jnp.float32) / half)
    ang = pos.astype(jnp.float32)[:, None] * inv_freq[None, :]
    cos = jnp.cos(ang)[None, :, None, :]
    sin = jnp.sin(ang)[None, :, None, :]
    x1 = x[..., :half].astype(jnp.float32)
    x2 = x[..., half:].astype(jnp.float32)
    return jnp.concatenate([x1 * cos - x2 * sin, x2 * cos + x1 * sin], axis=-1).astype(x.dtype)


def banded_attention(q, k, v, max_dist):
    N, L, KV, G, D = q.shape
    nb = L // BLOCK
    qb = (q * (1.0 / math.sqrt(D))).reshape(N, nb, BLOCK, KV, G, D)

    def with_prev(t):
        t = t.reshape(N, nb, BLOCK, KV, D)
        prev = jnp.pad(t, ((0, 0), (1, 0), (0, 0), (0, 0), (0, 0)))[:, :-1]
        return jnp.concatenate([prev, t], axis=2)

    kk = with_prev(k)
    vv = with_prev(v)
    s = jnp.einsum('nbqkgd,nbskd->nbkgqs', qb, kk, preferred_element_type=jnp.float32)
    qi = jnp.arange(BLOCK)[:, None]
    kj = jnp.arange(2 * BLOCK)[None, :]
    dist = qi + BLOCK - kj
    band = (dist >= 0) & (dist <= max_dist)
    key_pos = jnp.arange(nb)[:, None] * BLOCK + jnp.arange(2 * BLOCK)[None, :] - BLOCK
    valid = band[None, :, :] & (key_pos >= 0)[:, None, :]
    s = jnp.where(valid[None, :, None, None], s, -jnp.inf)
    mx = s.max(-1)
    p = jnp.exp(s - mx[..., None])
    l = p.sum(-1)
    o = jnp.einsum('nbkgqs,nbskd->nbqkgd', p, vv.astype(jnp.float32))
    mx = jnp.moveaxis(mx, -1, 2)
    l = jnp.moveaxis(l, -1, 2)
    o = o / l[..., None]
    return (o.reshape(N, L, KV, G, D), mx.reshape(N, L, KV, G), l.reshape(N, L, KV, G))


def dilated_mixture_attention(q, k, v):
    B, S, H, D = q.shape
    outs, maxes, dens = [], [], []
    for window, dil in A_PATTERNS:
        L = S // dil
        Lp = -(-L // BLOCK) * BLOCK

        def stride_gather(t):
            t = jnp.swapaxes(t.reshape(B, L, dil, H, D), 1, 2).reshape(B * dil, L, H, D)
            return jnp.pad(t, ((0, 0), (0, Lp - L), (0, 0), (0, 0)))

        def unstride(t):
            t = t[:, :L, :, 0]
            t = t.reshape((B, dil, L) + t.shape[2:])
            return jnp.swapaxes(t, 1, 2).reshape((B, S) + t.shape[3:])

        o, mx, l = banded_attention(stride_gather(q)[:, :, :, None], stride_gather(k),
                                    stride_gather(v), window // dil)
        outs.append(unstride(o))
        maxes.append(unstride(mx))
        dens.append(unstride(l))
    o = jnp.stack(outs)
    mx = jnp.stack(maxes)
    l = jnp.stack(dens)
    w = l * jnp.exp(mx - mx.max(0))
    return (w[..., None] * o).sum(0) / w.sum(0)[..., None]


def sink_window_attention(q, k, v, sinks):
    o, mx, l = banded_attention(q, k, v, B_WINDOW - 1)
    sink = sinks.astype(jnp.float32).reshape(B_KV_HEADS, B_GROUP)
    m_all = jnp.maximum(mx, sink)
    num = l * jnp.exp(mx - m_all)
    return o * (num / (num + jnp.exp(sink - m_all)))[..., None]


def token_mixer(x, w_in, b_in, sinks, g_mix_a, g_mix_b, w_out):
    B, S, _ = x.shape
    pos = jnp.arange(S)
    proj = jnp.einsum('bsd,de->bse', x, w_in) + b_in
    cuts = [A_WIDTH, 2 * A_WIDTH, 3 * A_WIDTH, 3 * A_WIDTH + B_WIDTH, 3 * A_WIDTH + B_WIDTH + B_KV_WIDTH]
    qa, ka, va, qb, kb, vb = jnp.split(proj, cuts, axis=-1)
    qa = rope(qa.reshape(B, S, A_HEADS, A_HEAD_DIM), pos)
    ka = rope(ka.reshape(B, S, A_HEADS, A_HEAD_DIM), pos)
    va = va.reshape(B, S, A_HEADS, A_HEAD_DIM)
    ya = dilated_mixture_attention(qa, ka, va).reshape(B, S, A_WIDTH)
    qb = rope(qb.reshape(B, S, B_HEADS, B_HEAD_DIM), pos).reshape(B, S, B_KV_HEADS, B_GROUP, B_HEAD_DIM)
    kb = rope(kb.reshape(B, S, B_KV_HEADS, B_HEAD_DIM), pos)
    vb = vb.reshape(B, S, B_KV_HEADS, B_HEAD_DIM)
    yb = sink_window_attention(qb, kb, vb, sinks).reshape(B, S, B_WIDTH)
    y = jnp.concatenate([rms_norm(ya, g_mix_a), rms_norm(yb, g_mix_b)], axis=-1).astype(x.dtype)
    return jnp.einsum('bse,ed->bsd', y, w_out)


def swiglu(x, w_up, w_down):
    gate, up = jnp.split(jnp.einsum('bsd,df->bsf', x, w_up), 2, axis=-1)
    return jnp.einsum('bsf,fd->bsd', jax.nn.silu(gate) * up, w_down)


def setup_inputs(seed: int = 0) -> dict:
    key = jax.random.key(seed)
    ks = jax.random.split(key, 14)
    f32 = jnp.float32
    x = jax.random.normal(ks[0], (BATCH, SEQ, D_MODEL), f32)
    col_scale = jnp.asarray(np.concatenate([
        np.ones(2 * A_WIDTH, np.float32), np.full(A_WIDTH, BETA, np.float32),
        np.ones(B_WIDTH + B_KV_WIDTH, np.float32), np.full(B_KV_WIDTH, BETA, np.float32)]))
    w_in = jax.random.normal(ks[1], (DEPTH, D_MODEL, IN_WIDTH), f32) * (D_MODEL ** -0.5) * col_scale
    b_in = 0.02 * jax.random.normal(ks[2], (DEPTH, IN_WIDTH), f32)
    sinks = 0.5 * jax.random.normal(ks[3], (DEPTH, B_HEADS), f32)
    g_mix_a = 1.0 + 0.02 * jax.random.normal(ks[4], (DEPTH, A_WIDTH), f32)
    g_mix_b = 1.0 + 0.02 * jax.random.normal(ks[5], (DEPTH, B_WIDTH), f32)
    w_out = jax.random.normal(ks[6], (DEPTH, MIX_WIDTH, D_MODEL), f32) * (MIX_WIDTH ** -0.5) * BETA
    ln1_g = 1.0 + 0.02 * jax.random.normal(ks[7], (DEPTH, D_MODEL), f32)
    ln1_b = 0.02 * jax.random.normal(ks[8], (DEPTH, D_MODEL), f32)
    w_up = jax.random.normal(ks[9], (DEPTH, D_MODEL, 2 * D_FF), f32) * (D_MODEL ** -0.5) * BETA
    w_down = jax.random.normal(ks[10], (DEPTH, D_FF, D_MODEL), f32) * (D_FF ** -0.5) * BETA
    ln2_g = 1.0 + 0.02 * jax.random.normal(ks[11], (DEPTH, D_MODEL), f32)
    ln2_b = 0.02 * jax.random.normal(ks[12], (DEPTH, D_MODEL), f32)
    return {"x": x, "w_in": w_in, "b_in": b_in, "sinks": sinks, "g_mix_a": g_mix_a,
            "g_mix_b": g_mix_b, "w_out": w_out, "ln1_g": ln1_g, "ln1_b": ln1_b,
            "w_up": w_up, "w_down": w_down, "ln2_g": ln2_g, "ln2_b": ln2_b}


def reference(x, w_in, b_in, sinks, g_mix_a, g_mix_b, w_out, ln1_g, ln1_b, w_up, w_down, ln2_g, ln2_b):
    for i in range(DEPTH):
        mix = token_mixer(x, w_in[i], b_in[i], sinks[i], g_mix_a[i], g_mix_b[i], w_out[i])
        x = layer_norm(ALPHA * x + mix, ln1_g[i], ln1_b[i])
        x = layer_norm(ALPHA * x + swiglu(x, w_up[i], w_down[i]), ln2_g[i], ln2_b[i])
    return x
```

```python
import functools
import math

import jax
import jax.numpy as jnp
from jax import lax
from jax.experimental import pallas as pl
from jax.experimental.pallas import tpu as pltpu

F32 = jnp.float32
BF16 = jnp.bfloat16

D_MODEL = 2048
A_HEAD_DIM = 128
A_WIDTH = 1024
A_HEADS = 8
A_DILATIONS = (1, 4, 16)
B_HEAD_DIM = 64
B_WIDTH = 1024
B_HEADS = 16
B_KV_HEADS = 2
B_GROUP = 8
B_KV_WIDTH = 128
QKV_A_WIDTH = 3 * A_WIDTH
QKV_B_WIDTH = B_WIDTH + 2 * B_KV_WIDTH
IN_WIDTH = QKV_A_WIDTH + QKV_B_WIDTH
D_FF = 5632
BLOCK = 128
ROPE_THETA = 10000.0
ALPHA = 2.0 ** 0.25
LN_EPS = 1e-5
RMS_EPS = 1e-6

LANES = 128
V7X_VMEM_BYTES = 64 * 1024 * 1024
V7X_VMEM_REQUEST_CAP = 60000 * 1024

PROJ_ROWS = 512
PROJ_CHUNK = 512
ATTN_ROWS = 512
OUT_ROWS = 512
FFN_ROWS = 512
FFN_CHUNK = 512


def _vmem_limit(estimate_bytes):
    return int(min(V7X_VMEM_REQUEST_CAP, estimate_bytes + estimate_bytes // 4))


def _params(semantics, vmem_estimate):
    return pltpu.CompilerParams(dimension_semantics=semantics,
                                vmem_limit_bytes=_vmem_limit(vmem_estimate))


def _rope_tables(seq, head_dim):
    half = head_dim // 2
    inv_freq = ROPE_THETA ** (-jnp.arange(half, dtype=F32) / half)
    ang = jnp.arange(seq).astype(F32)[:, None] * inv_freq[None, :]
    cos = jnp.cos(ang)
    sin = jnp.sin(ang)
    reps = LANES // head_dim
    cos_full = jnp.tile(jnp.concatenate([cos, cos], axis=-1), (1, reps))
    sin_signed = jnp.tile(jnp.concatenate([-sin, sin], axis=-1), (1, reps))
    return cos_full, sin_signed


def _proj_kernel(x_ref, w_ref, b_ref, ca_ref, sa_ref, cb_ref, sb_ref, oa_ref, ob_ref):
    xb = x_ref[...].astype(BF16)
    ca, sa, cb, sb = ca_ref[...], sa_ref[...], cb_ref[...], sb_ref[...]
    lane = lax.broadcasted_iota(jnp.int32, ca.shape, 1)
    first_half_b = (lane % B_HEAD_DIM) < (B_HEAD_DIM // 2)
    scale_a = 1.0 / math.sqrt(A_HEAD_DIM)
    scale_b = 1.0 / math.sqrt(B_HEAD_DIM)

    def rope_a(y):
        return y * ca + pltpu.roll(y, A_HEAD_DIM // 2, 1) * sa

    def rope_b(y):
        partner = jnp.where(first_half_b,
                            pltpu.roll(y, LANES - B_HEAD_DIM // 2, 1),
                            pltpu.roll(y, B_HEAD_DIM // 2, 1))
        return y * cb + partner * sb

    for c0 in range(0, IN_WIDTH, PROJ_CHUNK):
        c1 = min(c0 + PROJ_CHUNK, IN_WIDTH)
        acc = jnp.dot(xb, w_ref[:, c0:c1], preferred_element_type=F32) + b_ref[:, c0:c1]
        for col in range(c0, c1, LANES):
            y = acc[:, col - c0:col - c0 + LANES]
            if col < A_WIDTH:
                y = rope_a(y) * scale_a
            elif col < 2 * A_WIDTH:
                y = rope_a(y)
            elif col < QKV_A_WIDTH:
                pass
            elif col < QKV_A_WIDTH + B_WIDTH:
                y = rope_b(y) * scale_b
            elif col < QKV_A_WIDTH + B_WIDTH + B_KV_WIDTH:
                y = rope_b(y)
            if col < QKV_A_WIDTH:
                oa_ref[:, col:col + LANES] = y.astype(BF16)
            else:
                ob_ref[:, col - QKV_A_WIDTH:col - QKV_A_WIDTH + LANES] = y.astype(BF16)


def _proj_call(x2d, w_in, b_in):
    seq = x2d.shape[0]
    tm = PROJ_ROWS
    ca, sa = _rope_tables(seq, A_HEAD_DIM)
    cb, sb = _rope_tables(seq, B_HEAD_DIM)
    table_spec = pl.BlockSpec((tm, LANES), lambda i: (i, 0))
    vmem = (D_MODEL * IN_WIDTH * 2
            + 2 * tm * D_MODEL * 4
            + 2 * tm * IN_WIDTH * 2
            + 8 * tm * LANES * 4
            + tm * D_MODEL * 2 + 2 * tm * PROJ_CHUNK * 4)
    return pl.pallas_call(
        _proj_kernel,
        grid=(seq // tm,),
        in_specs=[
            pl.BlockSpec((tm, D_MODEL), lambda i: (i, 0)),
            pl.BlockSpec((D_MODEL, IN_WIDTH), lambda i: (0, 0), pipeline_mode=pl.Buffered(1)),
            pl.BlockSpec((1, IN_WIDTH), lambda i: (0, 0)),
            table_spec, table_spec, table_spec, table_spec,
        ],
        out_specs=[
            pl.BlockSpec((tm, QKV_A_WIDTH), lambda i: (i, 0)),
            pl.BlockSpec((tm, QKV_B_WIDTH), lambda i: (i, 0)),
        ],
        out_shape=[
            jax.ShapeDtypeStruct((seq, QKV_A_WIDTH), BF16),
            jax.ShapeDtypeStruct((seq, QKV_B_WIDTH), BF16),
        ],
        compiler_params=_params(("arbitrary",), vmem),
        name="proj_rope",
    )(x2d, w_in.astype(BF16), b_in.reshape(1, IN_WIDTH), ca, sa, cb, sb)


def _band_masks(min_dist, is_first_tile):
    qi = lax.broadcasted_iota(jnp.int32, (BLOCK, 2 * BLOCK), 0)
    kj = lax.broadcasted_iota(jnp.int32, (BLOCK, 2 * BLOCK), 1)
    band = (kj >= qi + min_dist) & (kj <= qi + BLOCK)
    first_lower = jnp.where(is_first_tile, BLOCK, 0)
    band_first = band & (kj >= first_lower)
    return band, band_first


def _rms_scale(sum_sq, width):
    return lax.rsqrt(sum_sq * (1.0 / width) + RMS_EPS)


def _attn_a_kernel(*refs, rows, has_state, final):
    it = iter(refs)
    q_ref, km_ref, kh_ref, vm_ref, vh_ref = (next(it) for _ in range(5))
    acc_in_ref = ml_in_ref = gain_ref = None
    if has_state:
        acc_in_ref, ml_in_ref = next(it), next(it)
    if final:
        gain_ref = next(it)
        y_ref = next(it)
    else:
        acc_out_ref, ml_out_ref = next(it), next(it)
    kbuf, vbuf = next(it), next(it)
    obuf = next(it) if final else None

    kbuf[0:BLOCK, :] = kh_ref[...]
    kbuf[BLOCK:, :] = km_ref[...]
    vbuf[0:BLOCK, :] = vh_ref[...]
    vbuf[BLOCK:, :] = vm_ref[...]

    band, band_first = _band_masks(0, pl.program_id(1) == 0)
    lane = lax.broadcasted_iota(jnp.int32, (BLOCK, LANES), 1)

    for j in range(rows // BLOCK):
        r0 = j * BLOCK
        valid = band_first if j == 0 else band
        ml_tile = jnp.zeros((BLOCK, LANES), F32)
        ml_prev = ml_in_ref[r0:r0 + BLOCK, :] if has_state else None
        sum_sq = jnp.zeros((BLOCK, 1), F32)
        for h in range(A_HEADS):
            c0 = h * A_HEAD_DIM
            q = q_ref[r0:r0 + BLOCK, c0:c0 + A_HEAD_DIM]
            k = kbuf[r0:r0 + 2 * BLOCK, c0:c0 + A_HEAD_DIM]
            v = vbuf[r0:r0 + 2 * BLOCK, c0:c0 + A_HEAD_DIM]
            s = lax.dot_general(q, k, (((1,), (1,)), ((), ())), preferred_element_type=F32)
            s = jnp.where(valid, s, -jnp.inf)
            m_new = jnp.max(s, axis=1, keepdims=True)
            if has_state:
                m_old = ml_prev[:, h:h + 1]
                l_old = ml_prev[:, A_HEADS + h:A_HEADS + h + 1]
                m_new = jnp.maximum(m_new, m_old)
                rescale = jnp.exp(m_old - m_new)
            p = jnp.exp(s - m_new)
            l_new = jnp.sum(p, axis=1, keepdims=True)
            acc = jnp.dot(p.astype(BF16), v, preferred_element_type=F32)
            if has_state:
                l_new = l_new + rescale * l_old
                acc = acc + rescale * acc_in_ref[r0:r0 + BLOCK, c0:c0 + A_HEAD_DIM]
            if final:
                o = acc * (1.0 / l_new)
                obuf[r0:r0 + BLOCK, c0:c0 + A_HEAD_DIM] = o
                sum_sq = sum_sq + jnp.sum(o * o, axis=1, keepdims=True)
            else:
                acc_out_ref[r0:r0 + BLOCK, c0:c0 + A_HEAD_DIM] = acc
                ml_tile = jnp.where(lane == h, m_new, ml_tile)
                ml_tile = jnp.where(lane == A_HEADS + h, l_new, ml_tile)
        if final:
            scale = _rms_scale(sum_sq, A_WIDTH)
            y_ref[r0:r0 + BLOCK, :] = (obuf[r0:r0 + BLOCK, :] * scale * gain_ref[...]).astype(BF16)
        else:
            ml_out_ref[r0:r0 + BLOCK, :] = ml_tile


def _attn_a_call(qkv_a, dil, state, gain):
    seq = qkv_a.shape[0]
    length = seq // dil
    rows = ATTN_ROWS
    blocks_per_tile = rows // BLOCK
    has_state = state is not None
    final = gain is not None

    qkv_view = qkv_a.reshape(length, dil * QKV_A_WIDTH)

    def tile_spec(which):
        return pl.BlockSpec((rows, A_WIDTH), lambda r, t: (t, 3 * r + which))

    def halo_spec(which):
        return pl.BlockSpec(
            (BLOCK, A_WIDTH),
            lambda r, t: (jnp.maximum(t * blocks_per_tile - 1, 0), 3 * r + which))

    acc_spec = pl.BlockSpec((rows, A_WIDTH), lambda r, t: (t, r))
    ml_spec = pl.BlockSpec((rows, LANES), lambda r, t: (t, r))

    in_specs = [tile_spec(0), tile_spec(1), halo_spec(1), tile_spec(2), halo_spec(2)]
    args = [qkv_view] * 5
    aliases = {}
    if has_state:
        acc, ml = state
        in_specs += [acc_spec, ml_spec]
        args += [acc.reshape(length, dil * A_WIDTH), ml.reshape(length, dil * LANES)]
        if not final:
            aliases = {5: 0, 6: 1}
    scratch = [pltpu.VMEM((rows + BLOCK, A_WIDTH), BF16), pltpu.VMEM((rows + BLOCK, A_WIDTH), BF16)]
    if final:
        in_specs.append(pl.BlockSpec((1, A_WIDTH), lambda r, t: (0, 0)))
        args.append(gain.reshape(1, A_WIDTH))
        out_specs = acc_spec
        out_shape = jax.ShapeDtypeStruct((length, dil * A_WIDTH), BF16)
        scratch.append(pltpu.VMEM((rows, A_WIDTH), F32))
    else:
        out_specs = [acc_spec, ml_spec]
        out_shape = [jax.ShapeDtypeStruct((length, dil * A_WIDTH), F32),
                     jax.ShapeDtypeStruct((length, dil * LANES), F32)]

    tile_bf16 = rows * A_WIDTH * 2
    vmem = (2 * 3 * tile_bf16 + 2 * 2 * BLOCK * A_WIDTH * 2
            + 2 * 2 * rows * (A_WIDTH + LANES) * 4
            + 2 * (rows + BLOCK) * A_WIDTH * 2 + rows * A_WIDTH * 4)
    out = pl.pallas_call(
        functools.partial(_attn_a_kernel, rows=rows, has_state=has_state, final=final),
        grid=(dil, length // rows),
        in_specs=in_specs,
        out_specs=out_specs,
        out_shape=out_shape,
        scratch_shapes=scratch,
        input_output_aliases=aliases,
        compiler_params=_params(("arbitrary", "arbitrary"), vmem),
        name=f"attn_a_d{dil}",
    )(*args)
    if final:
        return out.reshape(seq, A_WIDTH)
    return out[0].reshape(seq, A_WIDTH), out[1].reshape(seq, LANES)


def _attn_b_kernel(sinks_ref, q_ref, km_ref, kh_ref, vm_ref, vh_ref, gain_ref, y_ref,
                   klo, khi, vlo, vhi, obuf, *, rows):
    lane = lax.broadcasted_iota(jnp.int32, (rows + BLOCK, LANES), 1)
    low = lane < B_HEAD_DIM

    def fill(dst_lo, dst_hi, halo_ref, main_ref):
        full = jnp.concatenate([halo_ref[...], main_ref[...]], axis=0).astype(F32)
        swapped = pltpu.roll(full, B_HEAD_DIM, 1)
        zero = jnp.zeros_like(full)
        for c in range(B_KV_HEADS):
            own_low, own_high = (full, swapped) if c == 0 else (swapped, full)
            dst_lo[c] = jnp.where(low, own_low, zero).astype(BF16)
            dst_hi[c] = jnp.where(low, zero, own_high).astype(BF16)

    fill(klo, khi, kh_ref, km_ref)
    fill(vlo, vhi, vh_ref, vm_ref)

    band, band_first = _band_masks(1, pl.program_id(0) == 0)

    for j in range(rows // BLOCK):
        r0 = j * BLOCK
        valid = band_first if j == 0 else band
        sum_sq = jnp.zeros((BLOCK, 1), F32)
        for pair in range(B_HEADS // 2):
            c = (2 * pair) // B_GROUP
            c0 = pair * LANES
            q = q_ref[r0:r0 + BLOCK, c0:c0 + LANES]
            o = jnp.zeros((BLOCK, LANES), F32)
            for odd, (k_sel, v_sel) in enumerate(((klo, vlo), (khi, vhi))):
                k = k_sel[c, r0:r0 + 2 * BLOCK, :]
                v = v_sel[c, r0:r0 + 2 * BLOCK, :]
                s = lax.dot_general(q, k, (((1,), (1,)), ((), ())), preferred_element_type=F32)
                s = jnp.where(valid, s, -jnp.inf)
                m = jnp.max(s, axis=1, keepdims=True)
                p = jnp.exp(s - m)
                l = jnp.sum(p, axis=1, keepdims=True)
                acc = jnp.dot(p.astype(BF16), v, preferred_element_type=F32)
                sink = sinks_ref[2 * pair + odd]
                m_all = jnp.maximum(m, sink)
                e = jnp.exp(m - m_all)
                denom = l * e + jnp.exp(sink - m_all)
                o = o + acc * (e / denom)
            obuf[r0:r0 + BLOCK, c0:c0 + LANES] = o
            sum_sq = sum_sq + jnp.sum(o * o, axis=1, keepdims=True)
        scale = _rms_scale(sum_sq, B_WIDTH)
        y_ref[r0:r0 + BLOCK, :] = (obuf[r0:r0 + BLOCK, :] * scale * gain_ref[...]).astype(BF16)


def _attn_b_call(qkv_b, sinks, gain):
    seq = qkv_b.shape[0]
    rows = ATTN_ROWS
    blocks_per_tile = rows // BLOCK
    k_col = B_WIDTH // B_KV_WIDTH
    v_col = k_col + 1

    def kv_tile(col):
        return pl.BlockSpec((rows, B_KV_WIDTH), lambda t: (t, col))

    def kv_halo(col):
        return pl.BlockSpec((BLOCK, B_KV_WIDTH),
                            lambda t: (jnp.maximum(t * blocks_per_tile - 1, 0), col))

    kv_scratch = pltpu.VMEM((B_KV_HEADS, rows + BLOCK, LANES), BF16)
    vmem = (2 * 2 * rows * B_WIDTH * 2 + 4 * 2 * (rows + BLOCK) * LANES * 2
            + 4 * B_KV_HEADS * (rows + BLOCK) * LANES * 2 + rows * B_WIDTH * 4
            + 4 * (rows + BLOCK) * LANES * 4)
    return pl.pallas_call(
        functools.partial(_attn_b_kernel, rows=rows),
        grid=(seq // rows,),
        in_specs=[
            pl.BlockSpec(memory_space=pltpu.SMEM),
            pl.BlockSpec((rows, B_WIDTH), lambda t: (t, 0)),
            kv_tile(k_col), kv_halo(k_col), kv_tile(v_col), kv_halo(v_col),
            pl.BlockSpec((1, B_WIDTH), lambda t: (0, 0)),
        ],
        out_specs=pl.BlockSpec((rows, B_WIDTH), lambda t: (t, 0)),
        out_shape=jax.ShapeDtypeStruct((seq, B_WIDTH), BF16),
        scratch_shapes=[kv_scratch, kv_scratch, kv_scratch, kv_scratch,
                        pltpu.VMEM((rows, B_WIDTH), F32)],
        compiler_params=_params(("arbitrary",), vmem),
        name="attn_b",
    )(sinks, qkv_b, qkv_b, qkv_b, qkv_b, qkv_b, gain.reshape(1, B_WIDTH))


def _layer_norm_rows(z, g, b):
    mu = jnp.mean(z, axis=-1, keepdims=True)
    zc = z - mu
    var = jnp.mean(zc * zc, axis=-1, keepdims=True)
    return zc * lax.rsqrt(var + LN_EPS) * g + b


def _out_kernel(ya_ref, yb_ref, w_ref, x_ref, g_ref, b_ref, h_ref):
    mix = jnp.dot(ya_ref[...], w_ref[0:A_WIDTH, :], preferred_element_type=F32)
    mix = mix + jnp.dot(yb_ref[...], w_ref[A_WIDTH:, :], preferred_element_type=F32)
    h_ref[...] = _layer_norm_rows(ALPHA * x_ref[...] + mix, g_ref[...], b_ref[...])


def _out_call(ya, yb, w_out, x2d, ln_g, ln_b):
    seq = x2d.shape[0]
    tm = OUT_ROWS
    row_spec = pl.BlockSpec((1, D_MODEL), lambda i: (0, 0))
    vmem = (D_MODEL * D_MODEL * 2 + 2 * 2 * tm * A_WIDTH * 2
            + 2 * 2 * tm * D_MODEL * 4 + 2 * tm * D_MODEL * 4)
    return pl.pallas_call(
        _out_kernel,
        grid=(seq // tm,),
        in_specs=[
            pl.BlockSpec((tm, A_WIDTH), lambda i: (i, 0)),
            pl.BlockSpec((tm, B_WIDTH), lambda i: (i, 0)),
            pl.BlockSpec((D_MODEL, D_MODEL), lambda i: (0, 0), pipeline_mode=pl.Buffered(1)),
            pl.BlockSpec((tm, D_MODEL), lambda i: (i, 0)),
            row_spec, row_spec,
        ],
        out_specs=pl.BlockSpec((tm, D_MODEL), lambda i: (i, 0)),
        out_shape=jax.ShapeDtypeStruct((seq, D_MODEL), F32),
        compiler_params=_params(("arbitrary",), vmem),
        name="out_proj_ln",
    )(ya, yb, w_out.astype(BF16), x2d, ln_g.reshape(1, D_MODEL), ln_b.reshape(1, D_MODEL))


def _ffn_kernel(h_ref, wg_ref, wu_ref, wd_ref, g_ref, b_ref, o_ref, hb_ref):
    j = pl.program_id(1)

    @pl.when(j == 0)
    def _():
        h = h_ref[...]
        hb_ref[...] = h.astype(BF16)
        o_ref[...] = ALPHA * h

    hb = hb_ref[...]
    gate = jnp.dot(hb, wg_ref[...], preferred_element_type=F32)
    up = jnp.dot(hb, wu_ref[...], preferred_element_type=F32)
    act = (gate * jax.nn.sigmoid(gate) * up).astype(BF16)
    o_ref[...] += jnp.dot(act, wd_ref[...], preferred_element_type=F32)

    @pl.when(j == pl.num_programs(1) - 1)
    def _():
        o_ref[...] = _layer_norm_rows(o_ref[...], g_ref[...], b_ref[...])


def _ffn_call(h, w_up, w_down, ln_g, ln_b):
    seq = h.shape[0]
    tm, tf = FFN_ROWS, FFN_CHUNK
    n_chunks = D_FF // tf
    w_up_b = w_up.astype(BF16)
    row_spec = pl.BlockSpec((1, D_MODEL), lambda i, j: (0, 0))
    vmem = (2 * tm * D_MODEL * 4 + tm * D_MODEL * 2 + 2 * tm * D_MODEL * 4
            + 2 * 3 * D_MODEL * tf * 2 + 3 * tm * tf * 4)
    return pl.pallas_call(
        _ffn_kernel,
        grid=(seq // tm, n_chunks),
        in_specs=[
            pl.BlockSpec((tm, D_MODEL), lambda i, j: (i, 0)),
            pl.BlockSpec((D_MODEL, tf), lambda i, j: (0, j)),
            pl.BlockSpec((D_MODEL, tf), lambda i, j: (0, n_chunks + j)),
            pl.BlockSpec((tf, D_MODEL), lambda i, j: (j, 0)),
            row_spec, row_spec,
        ],
        out_specs=pl.BlockSpec((tm, D_MODEL), lambda i, j: (i, 0)),
        out_shape=jax.ShapeDtypeStruct((seq, D_MODEL), F32),
        scratch_shapes=[pltpu.VMEM((tm, D_MODEL), BF16)],
        compiler_params=_params(("arbitrary", "arbitrary"), vmem),
        name="swiglu_ln",
    )(h, w_up_b, w_up_b, w_down.astype(BF16), ln_g.reshape(1, D_MODEL), ln_b.reshape(1, D_MODEL))


def _layer(x2d, w_in, b_in, sinks, g_mix_a, g_mix_b, w_out, ln1_g, ln1_b, w_up, w_down, ln2_g, ln2_b):
    qkv_a, qkv_b = _proj_call(x2d, w_in, b_in)
    state = None
    for dil in A_DILATIONS[:-1]:
        state = _attn_a_call(qkv_a, dil, state, None)
    ya = _attn_a_call(qkv_a, A_DILATIONS[-1], state, g_mix_a)
    yb = _attn_b_call(qkv_b, sinks, g_mix_b)
    h = _out_call(ya, yb, w_out, x2d, ln1_g, ln1_b)
    return _ffn_call(h, w_up, w_down, ln2_g, ln2_b)


def kernel(x, w_in, b_in, sinks, g_mix_a, g_mix_b, w_out, ln1_g, ln1_b, w_up, w_down, ln2_g, ln2_b):
    batch, seq, d_model = x.shape
    assert (batch, d_model) == (1, D_MODEL) and seq % (ATTN_ROWS * A_DILATIONS[-1]) == 0
    h = x.reshape(seq, d_model)
    for i in range(w_in.shape[0]):
        h = _layer(h, w_in[i], b_in[i], sinks[i], g_mix_a[i], g_mix_b[i], w_out[i],
                   ln1_g[i], ln1_b[i], w_up[i], w_down[i], ln2_g[i], ln2_b[i])
    return h.reshape(batch, seq, d_model)
```

```python
import functools
import math

import jax
import jax.numpy as jnp
from jax import lax
from jax.experimental import pallas as pl
from jax.experimental.pallas import tpu as pltpu

F32 = jnp.float32
BF16 = jnp.bfloat16

D_MODEL = 2048
A_HEAD_DIM = 128
A_WIDTH = 1024
A_HEADS = 8
MID_DIL = 4
FAR_DIL = 16
B_HEAD_DIM = 64
B_WIDTH = 1024
B_HEADS = 16
B_KV_HEADS = 2
B_GROUP = 8
B_KV_WIDTH = 128
QKV_A_WIDTH = 3 * A_WIDTH
QKV_B_WIDTH = B_WIDTH + 2 * B_KV_WIDTH
IN_WIDTH = QKV_A_WIDTH + QKV_B_WIDTH
D_FF = 5632
BLOCK = 128
ROPE_THETA = 10000.0
ALPHA = 2.0 ** 0.25
LN_EPS = 1e-5
RMS_EPS = 1e-6

LANES = 128
V7X_VMEM_REQUEST_CAP = 60000 * 1024

PROJ_ROWS = 512
PROJ_CHUNK = 512
A_TILE = FAR_DIL * BLOCK
B_ROWS = 512
OUT_ROWS = 512
FFN_ROWS = 512
FFN_CHUNK = 512

A_MID_ROWS = A_TILE // MID_DIL
NEAR_SLAB = BLOCK // MID_DIL


def _vmem_limit(estimate_bytes):
    return int(min(V7X_VMEM_REQUEST_CAP, estimate_bytes + estimate_bytes // 4))


def _params(semantics, vmem_estimate):
    return pltpu.CompilerParams(dimension_semantics=semantics,
                                vmem_limit_bytes=_vmem_limit(vmem_estimate))


def _rope_tables(seq, head_dim):
    half = head_dim // 2
    inv_freq = ROPE_THETA ** (-jnp.arange(half, dtype=F32) / half)
    ang = jnp.arange(seq).astype(F32)[:, None] * inv_freq[None, :]
    cos = jnp.cos(ang)
    sin = jnp.sin(ang)
    reps = LANES // head_dim
    cos_full = jnp.tile(jnp.concatenate([cos, cos], axis=-1), (1, reps))
    sin_signed = jnp.tile(jnp.concatenate([-sin, sin], axis=-1), (1, reps))
    return cos_full, sin_signed


def _proj_kernel(x_ref, w_ref, b_ref, ca_ref, sa_ref, cb_ref, sb_ref,
                 o4_ref, o16_ref, ob_ref, nat_buf, mid_buf):
    rows = x_ref.shape[0]
    xb = x_ref[...].astype(BF16)
    ca, sa, cb, sb = ca_ref[...], sa_ref[...], cb_ref[...], sb_ref[...]
    lane = lax.broadcasted_iota(jnp.int32, ca.shape, 1)
    first_half_b = (lane % B_HEAD_DIM) < (B_HEAD_DIM // 2)
    scale_a = 1.0 / math.sqrt(A_HEAD_DIM)
    scale_b = 1.0 / math.sqrt(B_HEAD_DIM)
    mid_rows = rows // MID_DIL
    far_rows = rows // FAR_DIL

    def rope_a(y):
        return y * ca + pltpu.roll(y, A_HEAD_DIM // 2, 1) * sa

    def rope_b(y):
        partner = jnp.where(first_half_b,
                            pltpu.roll(y, LANES - B_HEAD_DIM // 2, 1),
                            pltpu.roll(y, B_HEAD_DIM // 2, 1))
        return y * cb + partner * sb

    for c0 in range(0, IN_WIDTH, PROJ_CHUNK):
        c1 = min(c0 + PROJ_CHUNK, IN_WIDTH)
        acc = jnp.dot(xb, w_ref[:, c0:c1], preferred_element_type=F32) + b_ref[:, c0:c1]
        for g, col in enumerate(range(c0, c1, LANES)):
            y = acc[:, col - c0:col - c0 + LANES]
            if col < A_WIDTH:
                y = rope_a(y) * scale_a
            elif col < 2 * A_WIDTH:
                y = rope_a(y)
            elif col < QKV_A_WIDTH:
                pass
            elif col < QKV_A_WIDTH + B_WIDTH:
                y = rope_b(y) * scale_b
            elif col < QKV_A_WIDTH + B_WIDTH + B_KV_WIDTH:
                y = rope_b(y)
            if col >= QKV_A_WIDTH:
                ob_ref[:, col - QKV_A_WIDTH:col - QKV_A_WIDTH + LANES] = y.astype(BF16)
                continue
            nat_buf[g] = y
            for c in range(MID_DIL):
                cls = nat_buf[g, pl.ds(c, mid_rows, stride=MID_DIL), :]
                o4_ref[c, :, col:col + LANES] = cls.astype(BF16)
                mid_buf[g, c * mid_rows:(c + 1) * mid_rows, :] = cls
            for c in range(MID_DIL):
                for c2 in range(FAR_DIL // MID_DIL):
                    cls = mid_buf[g, pl.ds(c * mid_rows + c2, far_rows, stride=MID_DIL), :]
                    o16_ref[c + MID_DIL * c2, :, col:col + LANES] = cls.astype(BF16)


def _proj_call(x2d, w_in, b_in):
    seq = x2d.shape[0]
    tm = PROJ_ROWS
    ca, sa = _rope_tables(seq, A_HEAD_DIM)
    cb, sb = _rope_tables(seq, B_HEAD_DIM)
    table_spec = pl.BlockSpec((tm, LANES), lambda i: (i, 0))
    groups = PROJ_CHUNK // LANES
    vmem = (D_MODEL * IN_WIDTH * 2
            + 2 * tm * D_MODEL * 4
            + 2 * tm * (2 * QKV_A_WIDTH + QKV_B_WIDTH) * 2
            + 8 * tm * LANES * 4
            + 2 * groups * tm * LANES * 4
            + tm * D_MODEL * 2 + 2 * tm * PROJ_CHUNK * 4)
    return pl.pallas_call(
        _proj_kernel,
        grid=(seq // tm,),
        in_specs=[
            pl.BlockSpec((tm, D_MODEL), lambda i: (i, 0)),
            pl.BlockSpec((D_MODEL, IN_WIDTH), lambda i: (0, 0), pipeline_mode=pl.Buffered(1)),
            pl.BlockSpec((1, IN_WIDTH), lambda i: (0, 0)),
            table_spec, table_spec, table_spec, table_spec,
        ],
        out_specs=[
            pl.BlockSpec((MID_DIL, tm // MID_DIL, QKV_A_WIDTH), lambda i: (0, i, 0)),
            pl.BlockSpec((FAR_DIL, tm // FAR_DIL, QKV_A_WIDTH), lambda i: (0, i, 0)),
            pl.BlockSpec((tm, QKV_B_WIDTH), lambda i: (i, 0)),
        ],
        out_shape=[
            jax.ShapeDtypeStruct((MID_DIL, seq // MID_DIL, QKV_A_WIDTH), BF16),
            jax.ShapeDtypeStruct((FAR_DIL, seq // FAR_DIL, QKV_A_WIDTH), BF16),
            jax.ShapeDtypeStruct((seq, QKV_B_WIDTH), BF16),
        ],
        scratch_shapes=[pltpu.VMEM((groups, tm, LANES), F32), pltpu.VMEM((groups, tm, LANES), F32)],
        compiler_params=_params(("arbitrary",), vmem),
        name="proj_rope",
    )(x2d, w_in.astype(BF16), b_in.reshape(1, IN_WIDTH), ca, sa, cb, sb)


def _band_masks(lo, is_first_tile):
    qi = lax.broadcasted_iota(jnp.int32, (BLOCK, 2 * BLOCK), 0)
    kj = lax.broadcasted_iota(jnp.int32, (BLOCK, 2 * BLOCK), 1)
    band = (kj >= qi + lo) & (kj <= qi + BLOCK)
    first_lower = jnp.where(is_first_tile, BLOCK, 0)
    return band, band & (kj >= first_lower)


def _near_masks(is_first_tile):
    iq = lax.broadcasted_iota(jnp.int32, (BLOCK, 2 * BLOCK), 0)
    ik = lax.broadcasted_iota(jnp.int32, (BLOCK, 2 * BLOCK), 1)
    cq, mq = iq // NEAR_SLAB, iq % NEAR_SLAB
    ck, mk = ik // (2 * NEAR_SLAB), ik % (2 * NEAR_SLAB)
    dist = MID_DIL * (mq + NEAR_SLAB - mk) + (cq - ck)
    valid = (dist >= 0) & (dist <= BLOCK)
    first_lower = jnp.where(is_first_tile, NEAR_SLAB, 0)
    return valid, valid & (mk >= first_lower)


def _rms_scale(sum_sq, width):
    return lax.rsqrt(sum_sq * (1.0 / width) + RMS_EPS)


def _scores(q, k, valid):
    s = lax.dot_general(q, k, (((1,), (1,)), ((), ())), preferred_element_type=F32)
    return jnp.where(valid, s, -jnp.inf)


def _attn_a_kernel(q4_ref, k4_ref, k4h_ref, v4_ref, v4h_ref,
                   q16_ref, k16_ref, k16h_ref, v16_ref, v16h_ref, y_ref,
                   k4buf, v4buf, k16buf, v16buf, acc_s, m_s, l_s, y_s):
    first = pl.program_id(0) == 0
    k4buf[:, 0:BLOCK, :] = k4h_ref[...]
    k4buf[:, BLOCK:, :] = k4_ref[...]
    v4buf[:, 0:BLOCK, :] = v4h_ref[...]
    v4buf[:, BLOCK:, :] = v4_ref[...]
    k16buf[:, 0:BLOCK, :] = k16h_ref[...]
    k16buf[:, BLOCK:, :] = k16_ref[...]
    v16buf[:, 0:BLOCK, :] = v16h_ref[...]
    v16buf[:, BLOCK:, :] = v16_ref[...]

    near, near_first = _near_masks(first)
    band, band_first = _band_masks(0, first)
    full = (BLOCK, LANES)

    for j in range(A_TILE // BLOCK):
        r0 = j * NEAR_SLAB
        h0 = BLOCK + r0 - NEAR_SLAB
        q = jnp.concatenate([q4_ref[c, r0:r0 + NEAR_SLAB, :] for c in range(MID_DIL)], axis=0)
        k = jnp.concatenate([k4buf[c, h0:h0 + 2 * NEAR_SLAB, :] for c in range(MID_DIL)], axis=0)
        v = jnp.concatenate([v4buf[c, h0:h0 + 2 * NEAR_SLAB, :] for c in range(MID_DIL)], axis=0)
        s = _scores(q, k, near_first if j == 0 else near)
        m = jnp.max(s, axis=1, keepdims=True)
        p = jnp.exp(s - m)
        l = jnp.sum(p, axis=1, keepdims=True)
        acc = jnp.dot(p.astype(BF16), v, preferred_element_type=F32)
        m_b = jnp.broadcast_to(m, full)
        l_b = jnp.broadcast_to(l, full)
        for c in range(MID_DIL):
            dst = slice(c * A_MID_ROWS + r0, c * A_MID_ROWS + r0 + NEAR_SLAB)
            src = slice(c * NEAR_SLAB, (c + 1) * NEAR_SLAB)
            acc_s[dst, :] = acc[src]
            m_s[dst, :] = m_b[src]
            l_s[dst, :] = l_b[src]

    def fold(q, k, v, valid, rows):
        s = _scores(q, k, valid)
        m_old = m_s[rows, :][:, :1]
        l_old = l_s[rows, :][:, :1]
        m_new = jnp.maximum(jnp.max(s, axis=1, keepdims=True), m_old)
        rescale = jnp.exp(m_old - m_new)
        p = jnp.exp(s - m_new)
        l_new = rescale * l_old + jnp.sum(p, axis=1, keepdims=True)
        acc = rescale * acc_s[rows, :] + jnp.dot(p.astype(BF16), v, preferred_element_type=F32)
        return acc, m_new, l_new

    for c in range(MID_DIL):
        for j in range(A_MID_ROWS // BLOCK):
            r0 = j * BLOCK
            rows = slice(c * A_MID_ROWS + r0, c * A_MID_ROWS + r0 + BLOCK)
            acc, m_new, l_new = fold(q4_ref[c, r0:r0 + BLOCK, :], k4buf[c, r0:r0 + 2 * BLOCK, :],
                                     v4buf[c, r0:r0 + 2 * BLOCK, :],
                                     band_first if j == 0 else band, rows)
            acc_s[rows, :] = acc
            m_s[rows, :] = jnp.broadcast_to(m_new, full)
            l_s[rows, :] = jnp.broadcast_to(l_new, full)

    for r in range(FAR_DIL):
        c, c2 = r % MID_DIL, r // MID_DIL
        rows = pl.ds(c * A_MID_ROWS + c2, BLOCK, stride=MID_DIL)
        acc, _, l_new = fold(q16_ref[r], k16buf[r], v16buf[r], band_first, rows)
        acc_s[rows, :] = acc * (1.0 / l_new)

    for c in range(MID_DIL):
        y_s[pl.ds(c, A_MID_ROWS, stride=MID_DIL), :] = acc_s[c * A_MID_ROWS:(c + 1) * A_MID_ROWS, :]
    y_ref[...] = y_s[...].astype(BF16)


def _attn_a_call(qkv4, qkv16):
    seq = qkv4.shape[0] * qkv4.shape[1]
    tiles = seq // A_TILE
    mid_blocks = A_MID_ROWS // BLOCK

    def mid_tile(which):
        return pl.BlockSpec((MID_DIL, A_MID_ROWS, A_HEAD_DIM),
                            lambda t, h: (0, t, which * A_HEADS + h))

    def mid_halo(which):
        return pl.BlockSpec((MID_DIL, BLOCK, A_HEAD_DIM),
                            lambda t, h: (0, jnp.maximum(t * mid_blocks - 1, 0), which * A_HEADS + h))

    def far_tile(which):
        return pl.BlockSpec((FAR_DIL, BLOCK, A_HEAD_DIM), lambda t, h: (0, t, which * A_HEADS + h))

    def far_halo(which):
        return pl.BlockSpec((FAR_DIL, BLOCK, A_HEAD_DIM),
                            lambda t, h: (0, jnp.maximum(t - 1, 0), which * A_HEADS + h))

    slab = A_TILE * A_HEAD_DIM
    vmem = (2 * (3 * slab + 2 * slab // 4) * 2
            + 2 * 5 * slab * 2
            + 2 * slab * 2
            + 2 * (slab + slab // 4) * 2 + 2 * 2 * slab * 2
            + 4 * slab * 4)
    return pl.pallas_call(
        _attn_a_kernel,
        grid=(tiles, A_HEADS),
        in_specs=[mid_tile(0), mid_tile(1), mid_halo(1), mid_tile(2), mid_halo(2),
                  far_tile(0), far_tile(1), far_halo(1), far_tile(2), far_halo(2)],
        out_specs=pl.BlockSpec((A_TILE, A_HEAD_DIM), lambda t, h: (t, h)),
        out_shape=jax.ShapeDtypeStruct((seq, A_WIDTH), BF16),
        scratch_shapes=[
            pltpu.VMEM((MID_DIL, A_MID_ROWS + BLOCK, A_HEAD_DIM), BF16),
            pltpu.VMEM((MID_DIL, A_MID_ROWS + BLOCK, A_HEAD_DIM), BF16),
            pltpu.VMEM((FAR_DIL, 2 * BLOCK, A_HEAD_DIM), BF16),
            pltpu.VMEM((FAR_DIL, 2 * BLOCK, A_HEAD_DIM), BF16),
            pltpu.VMEM((A_TILE, LANES), F32),
            pltpu.VMEM((A_TILE, LANES), F32),
            pltpu.VMEM((A_TILE, LANES), F32),
            pltpu.VMEM((A_TILE, LANES), F32),
        ],
        compiler_params=_params(("arbitrary", "arbitrary"), vmem),
        name="attn_a",
    )(qkv4, qkv4, qkv4, qkv4, qkv4, qkv16, qkv16, qkv16, qkv16, qkv16)


def _attn_b_kernel(sinks_ref, q_ref, km_ref, kh_ref, vm_ref, vh_ref, gain_ref, y_ref,
                   klo, khi, vlo, vhi, obuf, *, rows):
    lane = lax.broadcasted_iota(jnp.int32, (rows + BLOCK, LANES), 1)
    low = lane < B_HEAD_DIM

    def fill(dst_lo, dst_hi, halo_ref, main_ref):
        full = jnp.concatenate([halo_ref[...], main_ref[...]], axis=0).astype(F32)
        swapped = pltpu.roll(full, B_HEAD_DIM, 1)
        zero = jnp.zeros_like(full)
        for c in range(B_KV_HEADS):
            own_low, own_high = (full, swapped) if c == 0 else (swapped, full)
            dst_lo[c] = jnp.where(low, own_low, zero).astype(BF16)
            dst_hi[c] = jnp.where(low, zero, own_high).astype(BF16)

    fill(klo, khi, kh_ref, km_ref)
    fill(vlo, vhi, vh_ref, vm_ref)

    band, band_first = _band_masks(1, pl.program_id(0) == 0)

    for j in range(rows // BLOCK):
        r0 = j * BLOCK
        valid = band_first if j == 0 else band
        sum_sq = jnp.zeros((BLOCK, 1), F32)
        for pair in range(B_HEADS // 2):
            c = (2 * pair) // B_GROUP
            c0 = pair * LANES
            q = q_ref[r0:r0 + BLOCK, c0:c0 + LANES]
            o = jnp.zeros((BLOCK, LANES), F32)
            for odd, (k_sel, v_sel) in enumerate(((klo, vlo), (khi, vhi))):
                s = _scores(q, k_sel[c, r0:r0 + 2 * BLOCK, :], valid)
                m = jnp.max(s, axis=1, keepdims=True)
                p = jnp.exp(s - m)
                l = jnp.sum(p, axis=1, keepdims=True)
                acc = jnp.dot(p.astype(BF16), v_sel[c, r0:r0 + 2 * BLOCK, :],
                              preferred_element_type=F32)
                sink = sinks_ref[2 * pair + odd]
                m_all = jnp.maximum(m, sink)
                e = jnp.exp(m - m_all)
                denom = l * e + jnp.exp(sink - m_all)
                o = o + acc * (e / denom)
            obuf[r0:r0 + BLOCK, c0:c0 + LANES] = o
            sum_sq = sum_sq + jnp.sum(o * o, axis=1, keepdims=True)
        scale = _rms_scale(sum_sq, B_WIDTH)
        y_ref[r0:r0 + BLOCK, :] = (obuf[r0:r0 + BLOCK, :] * scale * gain_ref[...]).astype(BF16)


def _attn_b_call(qkv_b, sinks, gain):
    seq = qkv_b.shape[0]
    rows = B_ROWS
    blocks_per_tile = rows // BLOCK
    k_col = B_WIDTH // B_KV_WIDTH
    v_col = k_col + 1

    def kv_tile(col):
        return pl.BlockSpec((rows, B_KV_WIDTH), lambda t: (t, col))

    def kv_halo(col):
        return pl.BlockSpec((BLOCK, B_KV_WIDTH),
                            lambda t: (jnp.maximum(t * blocks_per_tile - 1, 0), col))

    kv_scratch = pltpu.VMEM((B_KV_HEADS, rows + BLOCK, LANES), BF16)
    vmem = (2 * 2 * rows * B_WIDTH * 2 + 4 * 2 * (rows + BLOCK) * LANES * 2
            + 4 * B_KV_HEADS * (rows + BLOCK) * LANES * 2 + rows * B_WIDTH * 4
            + 4 * (rows + BLOCK) * LANES * 4)
    return pl.pallas_call(
        functools.partial(_attn_b_kernel, rows=rows),
        grid=(seq // rows,),
        in_specs=[
            pl.BlockSpec(memory_space=pltpu.SMEM),
            pl.BlockSpec((rows, B_WIDTH), lambda t: (t, 0)),
            kv_tile(k_col), kv_halo(k_col), kv_tile(v_col), kv_halo(v_col),
            pl.BlockSpec((1, B_WIDTH), lambda t: (0, 0)),
        ],
        out_specs=pl.BlockSpec((rows, B_WIDTH), lambda t: (t, 0)),
        out_shape=jax.ShapeDtypeStruct((seq, B_WIDTH), BF16),
        scratch_shapes=[kv_scratch, kv_scratch, kv_scratch, kv_scratch,
                        pltpu.VMEM((rows, B_WIDTH), F32)],
        compiler_params=_params(("arbitrary",), vmem),
        name="attn_b",
    )(sinks, qkv_b, qkv_b, qkv_b, qkv_b, qkv_b, gain.reshape(1, B_WIDTH))


def _layer_norm_rows(z, g, b):
    mu = jnp.mean(z, axis=-1, keepdims=True)
    zc = z - mu
    var = jnp.mean(zc * zc, axis=-1, keepdims=True)
    return zc * lax.rsqrt(var + LN_EPS) * g + b


def _out_kernel(ya_ref, yb_ref, ga_ref, w_ref, x_ref, g_ref, b_ref, h_ref):
    ya = ya_ref[...].astype(F32)
    scale = _rms_scale(jnp.sum(ya * ya, axis=1, keepdims=True), A_WIDTH)
    ya = (ya * scale * ga_ref[...]).astype(BF16)
    mix = jnp.dot(ya, w_ref[0:A_WIDTH, :], preferred_element_type=F32)
    mix = mix + jnp.dot(yb_ref[...], w_ref[A_WIDTH:, :], preferred_element_type=F32)
    h_ref[...] = _layer_norm_rows(ALPHA * x_ref[...] + mix, g_ref[...], b_ref[...])


def _out_call(ya, yb, gain_a, w_out, x2d, ln_g, ln_b):
    seq = x2d.shape[0]
    tm = OUT_ROWS
    row_spec = pl.BlockSpec((1, D_MODEL), lambda i: (0, 0))
    vmem = (D_MODEL * D_MODEL * 2 + 2 * 2 * tm * A_WIDTH * 2
            + 2 * 2 * tm * D_MODEL * 4 + 2 * tm * D_MODEL * 4)
    return pl.pallas_call(
        _out_kernel,
        grid=(seq // tm,),
        in_specs=[
            pl.BlockSpec((tm, A_WIDTH), lambda i: (i, 0)),
            pl.BlockSpec((tm, B_WIDTH), lambda i: (i, 0)),
            pl.BlockSpec((1, A_WIDTH), lambda i: (0, 0)),
            pl.BlockSpec((D_MODEL, D_MODEL), lambda i: (0, 0), pipeline_mode=pl.Buffered(1)),
            pl.BlockSpec((tm, D_MODEL), lambda i: (i, 0)),
            row_spec, row_spec,
        ],
        out_specs=pl.BlockSpec((tm, D_MODEL), lambda i: (i, 0)),
        out_shape=jax.ShapeDtypeStruct((seq, D_MODEL), F32),
        compiler_params=_params(("arbitrary",), vmem),
        name="out_proj_ln",
    )(ya, yb, gain_a.reshape(1, A_WIDTH), w_out.astype(BF16), x2d,
      ln_g.reshape(1, D_MODEL), ln_b.reshape(1, D_MODEL))


def _ffn_kernel(h_ref, wg_ref, wu_ref, wd_ref, g_ref, b_ref, o_ref, hb_ref):
    j = pl.program_id(1)

    @pl.when(j == 0)
    def _():
        h = h_ref[...]
        hb_ref[...] = h.astype(BF16)
        o_ref[...] = ALPHA * h

    hb = hb_ref[...]
    gate = jnp.dot(hb, wg_ref[...], preferred_element_type=F32)
    up = jnp.dot(hb, wu_ref[...], preferred_element_type=F32)
    act = (gate * jax.nn.sigmoid(gate) * up).astype(BF16)
    o_ref[...] += jnp.dot(act, wd_ref[...], preferred_element_type=F32)

    @pl.when(j == pl.num_programs(1) - 1)
    def _():
        o_ref[...] = _layer_norm_rows(o_ref[...], g_ref[...], b_ref[...])


def _ffn_call(h, w_up, w_down, ln_g, ln_b):
    seq = h.shape[0]
    tm, tf = FFN_ROWS, FFN_CHUNK
    n_chunks = D_FF // tf
    w_up_b = w_up.astype(BF16)
    row_spec = pl.BlockSpec((1, D_MODEL), lambda i, j: (0, 0))
    vmem = (2 * tm * D_MODEL * 4 + tm * D_MODEL * 2 + 2 * tm * D_MODEL * 4
            + 2 * 3 * D_MODEL * tf * 2 + 3 * tm * tf * 4)
    return pl.pallas_call(
        _ffn_kernel,
        grid=(seq // tm, n_chunks),
        in_specs=[
            pl.BlockSpec((tm, D_MODEL), lambda i, j: (i, 0)),
            pl.BlockSpec((D_MODEL, tf), lambda i, j: (0, j)),
            pl.BlockSpec((D_MODEL, tf), lambda i, j: (0, n_chunks + j)),
            pl.BlockSpec((tf, D_MODEL), lambda i, j: (j, 0)),
            row_spec, row_spec,
        ],
        out_specs=pl.BlockSpec((tm, D_MODEL), lambda i, j: (i, 0)),
        out_shape=jax.ShapeDtypeStruct((seq, D_MODEL), F32),
        scratch_shapes=[pltpu.VMEM((tm, D_MODEL), BF16)],
        compiler_params=_params(("arbitrary", "arbitrary"), vmem),
        name="swiglu_ln",
    )(h, w_up_b, w_up_b, w_down.astype(BF16), ln_g.reshape(1, D_MODEL), ln_b.reshape(1, D_MODEL))


def _layer(x2d, w_in, b_in, sinks, g_mix_a, g_mix_b, w_out, ln1_g, ln1_b, w_up, w_down, ln2_g, ln2_b):
    qkv4, qkv16, qkv_b = _proj_call(x2d, w_in, b_in)
    ya = _attn_a_call(qkv4, qkv16)
    yb = _attn_b_call(qkv_b, sinks, g_mix_b)
    h = _out_call(ya, yb, g_mix_a, w_out, x2d, ln1_g, ln1_b)
    return _ffn_call(h, w_up, w_down, ln2_g, ln2_b)


def kernel(x, w_in, b_in, sinks, g_mix_a, g_mix_b, w_out, ln1_g, ln1_b, w_up, w_down, ln2_g, ln2_b):
    batch, seq, d_model = x.shape
    assert (batch, d_model) == (1, D_MODEL) and seq % A_TILE == 0
    h = x.reshape(seq, d_model)
    for i in range(w_in.shape[0]):
        h = _layer(h, w_in[i], b_in[i], sinks[i], g_mix_a[i], g_mix_b[i], w_out[i],
                   ln1_g[i], ln1_b[i], w_up[i], w_down[i], ln2_g[i], ln2_b[i])
    return h.reshape(batch, seq, d_model)
```

```python
import functools
import math

import jax
import jax.numpy as jnp
from jax import lax
from jax.experimental import pallas as pl
from jax.experimental.pallas import tpu as pltpu

F32 = jnp.float32
BF16 = jnp.bfloat16

D_MODEL = 2048
A_HEAD_DIM = 128
A_WIDTH = 1024
A_HEADS = 8
MID_DIL = 4
FAR_DIL = 16
B_HEAD_DIM = 64
B_WIDTH = 1024
B_HEADS = 16
B_KV_HEADS = 2
B_GROUP = 8
B_KV_WIDTH = 128
QKV_A_WIDTH = 3 * A_WIDTH
QKV_B_WIDTH = B_WIDTH + 2 * B_KV_WIDTH
IN_WIDTH = QKV_A_WIDTH + QKV_B_WIDTH
D_FF = 5632
BLOCK = 128
ROPE_THETA = 10000.0
ALPHA = 2.0 ** 0.25
LN_EPS = 1e-5
RMS_EPS = 1e-6

LANES = 128
V7X_VMEM_REQUEST_CAP = 60000 * 1024

PROJ_ROWS = 512
PROJ_CHUNK = 512
A_TILE = FAR_DIL * BLOCK
B_ROWS = 512
OUT_ROWS = 512
FFN_ROWS = 512
FFN_CHUNK = 512

A_MID_ROWS = A_TILE // MID_DIL
NEAR_SLAB = BLOCK // MID_DIL
GROUP = 4


def _vmem_limit(estimate_bytes):
    return int(min(V7X_VMEM_REQUEST_CAP, estimate_bytes + estimate_bytes // 4))


def _params(semantics, vmem_estimate):
    return pltpu.CompilerParams(dimension_semantics=semantics,
                                vmem_limit_bytes=_vmem_limit(vmem_estimate))


def _rope_tables(seq, head_dim):
    half = head_dim // 2
    inv_freq = ROPE_THETA ** (-jnp.arange(half, dtype=F32) / half)
    ang = jnp.arange(seq).astype(F32)[:, None] * inv_freq[None, :]
    cos = jnp.cos(ang)
    sin = jnp.sin(ang)
    reps = LANES // head_dim
    cos_full = jnp.tile(jnp.concatenate([cos, cos], axis=-1), (1, reps))
    sin_signed = jnp.tile(jnp.concatenate([-sin, sin], axis=-1), (1, reps))
    return cos_full, sin_signed


def _proj_kernel(x_ref, w_ref, b_ref, ca_ref, sa_ref, cb_ref, sb_ref,
                 o4_ref, o16_ref, ob_ref, nat_buf, mid_buf):
    rows = x_ref.shape[0]
    xb = x_ref[...].astype(BF16)
    ca, sa, cb, sb = ca_ref[...], sa_ref[...], cb_ref[...], sb_ref[...]
    lane = lax.broadcasted_iota(jnp.int32, ca.shape, 1)
    first_half_b = (lane % B_HEAD_DIM) < (B_HEAD_DIM // 2)
    scale_a = 1.0 / math.sqrt(A_HEAD_DIM)
    scale_b = 1.0 / math.sqrt(B_HEAD_DIM)
    mid_rows = rows // MID_DIL
    far_rows = rows // FAR_DIL

    def rope_a(y):
        return y * ca + pltpu.roll(y, A_HEAD_DIM // 2, 1) * sa

    def rope_b(y):
        partner = jnp.where(first_half_b,
                            pltpu.roll(y, LANES - B_HEAD_DIM // 2, 1),
                            pltpu.roll(y, B_HEAD_DIM // 2, 1))
        return y * cb + partner * sb

    for c0 in range(0, IN_WIDTH, PROJ_CHUNK):
        c1 = min(c0 + PROJ_CHUNK, IN_WIDTH)
        acc = jnp.dot(xb, w_ref[:, c0:c1], preferred_element_type=F32) + b_ref[:, c0:c1]
        for g, col in enumerate(range(c0, c1, LANES)):
            y = acc[:, col - c0:col - c0 + LANES]
            if col < A_WIDTH:
                y = rope_a(y) * scale_a
            elif col < 2 * A_WIDTH:
                y = rope_a(y)
            elif col < QKV_A_WIDTH:
                pass
            elif col < QKV_A_WIDTH + B_WIDTH:
                y = rope_b(y) * scale_b
            elif col < QKV_A_WIDTH + B_WIDTH + B_KV_WIDTH:
                y = rope_b(y)
            if col >= QKV_A_WIDTH:
                ob_ref[:, col - QKV_A_WIDTH:col - QKV_A_WIDTH + LANES] = y.astype(BF16)
                continue
            nat_buf[g] = y
            for c in range(MID_DIL):
                cls = nat_buf[g, pl.ds(c, mid_rows, stride=MID_DIL), :]
                o4_ref[c, :, col:col + LANES] = cls.astype(BF16)
                mid_buf[g, c * mid_rows:(c + 1) * mid_rows, :] = cls
            for c in range(MID_DIL):
                for c2 in range(FAR_DIL // MID_DIL):
                    cls = mid_buf[g, pl.ds(c * mid_rows + c2, far_rows, stride=MID_DIL), :]
                    o16_ref[c + MID_DIL * c2, :, col:col + LANES] = cls.astype(BF16)


def _proj_call(x2d, w_in, b_in):
    seq = x2d.shape[0]
    tm = PROJ_ROWS
    ca, sa = _rope_tables(seq, A_HEAD_DIM)
    cb, sb = _rope_tables(seq, B_HEAD_DIM)
    table_spec = pl.BlockSpec((tm, LANES), lambda i: (i, 0))
    groups = PROJ_CHUNK // LANES
    vmem = (D_MODEL * IN_WIDTH * 2
            + 2 * tm * D_MODEL * 4
            + 2 * tm * (2 * QKV_A_WIDTH + QKV_B_WIDTH) * 2
            + 8 * tm * LANES * 4
            + 2 * groups * tm * LANES * 4
            + tm * D_MODEL * 2 + 2 * tm * PROJ_CHUNK * 4)
    return pl.pallas_call(
        _proj_kernel,
        grid=(seq // tm,),
        in_specs=[
            pl.BlockSpec((tm, D_MODEL), lambda i: (i, 0)),
            pl.BlockSpec((D_MODEL, IN_WIDTH), lambda i: (0, 0), pipeline_mode=pl.Buffered(1)),
            pl.BlockSpec((1, IN_WIDTH), lambda i: (0, 0)),
            table_spec, table_spec, table_spec, table_spec,
        ],
        out_specs=[
            pl.BlockSpec((MID_DIL, tm // MID_DIL, QKV_A_WIDTH), lambda i: (0, i, 0)),
            pl.BlockSpec((FAR_DIL, tm // FAR_DIL, QKV_A_WIDTH), lambda i: (0, i, 0)),
            pl.BlockSpec((tm, QKV_B_WIDTH), lambda i: (i, 0)),
        ],
        out_shape=[
            jax.ShapeDtypeStruct((MID_DIL, seq // MID_DIL, QKV_A_WIDTH), BF16),
            jax.ShapeDtypeStruct((FAR_DIL, seq // FAR_DIL, QKV_A_WIDTH), BF16),
            jax.ShapeDtypeStruct((seq, QKV_B_WIDTH), BF16),
        ],
        scratch_shapes=[pltpu.VMEM((groups, tm, LANES), F32), pltpu.VMEM((groups, tm, LANES), F32)],
        compiler_params=_params(("arbitrary",), vmem),
        name="proj_rope",
    )(x2d, w_in.astype(BF16), b_in.reshape(1, IN_WIDTH), ca, sa, cb, sb)


def _band_masks(lo, is_first_tile):
    qi = lax.broadcasted_iota(jnp.int32, (BLOCK, 2 * BLOCK), 0)
    kj = lax.broadcasted_iota(jnp.int32, (BLOCK, 2 * BLOCK), 1)
    band = (kj >= qi + lo) & (kj <= qi + BLOCK)
    first_lower = jnp.where(is_first_tile, BLOCK, 0)
    return band, band & (kj >= first_lower)


def _near_masks(is_first_tile):
    iq = lax.broadcasted_iota(jnp.int32, (BLOCK, 2 * BLOCK), 0)
    ik = lax.broadcasted_iota(jnp.int32, (BLOCK, 2 * BLOCK), 1)
    cq, mq = iq // NEAR_SLAB, iq % NEAR_SLAB
    ck, mk = ik // (2 * NEAR_SLAB), ik % (2 * NEAR_SLAB)
    dist = MID_DIL * (mq + NEAR_SLAB - mk) + (cq - ck)
    valid = (dist >= 0) & (dist <= BLOCK)
    first_lower = jnp.where(is_first_tile, NEAR_SLAB, 0)
    return valid, valid & (mk >= first_lower)


def _rms_scale(sum_sq, width):
    return lax.rsqrt(sum_sq * (1.0 / width) + RMS_EPS)


def _scores(q, k, valid):
    s = lax.dot_general(q, k, (((1,), (1,)), ((), ())), preferred_element_type=F32)
    return jnp.where(valid, s, -jnp.inf)


def _attn_a_kernel(q4_ref, k4_ref, k4h_ref, v4_ref, v4h_ref,
                   q16_ref, k16_ref, k16h_ref, v16_ref, v16h_ref, y_ref,
                   k4buf, v4buf, k16buf, v16buf,
                   acc_n, m_n, l_n, acc_m, m_m, l_m, o_s, y_s):
    first = pl.program_id(0) == 0
    k4buf[:, 0:BLOCK, :] = k4h_ref[...]
    k4buf[:, BLOCK:, :] = k4_ref[...]
    v4buf[:, 0:BLOCK, :] = v4h_ref[...]
    v4buf[:, BLOCK:, :] = v4_ref[...]
    k16buf[:, 0:BLOCK, :] = k16h_ref[...]
    k16buf[:, BLOCK:, :] = k16_ref[...]
    v16buf[:, 0:BLOCK, :] = v16h_ref[...]
    v16buf[:, BLOCK:, :] = v16_ref[...]

    near, near_first = _near_masks(first)
    band, band_first = _band_masks(0, first)
    full = (BLOCK, LANES)
    ones = jnp.ones((2 * BLOCK, LANES), BF16)

    def fold_group(blocks, olds):
        scores = [_scores(q, k, valid) for q, k, _, valid in blocks]
        m_new, rescale, probs = [], [], []
        for i, s in enumerate(scores):
            m = jnp.max(s, axis=1, keepdims=True)
            if olds is not None:
                m = jnp.maximum(olds[i][0], m)
                rescale.append(jnp.exp(olds[i][0] - m))
                p = jnp.concatenate([jnp.exp(s[:, :LANES] - m), jnp.exp(s[:, LANES:] - m)], axis=1)
            else:
                p = jnp.exp(s - m)
                m = jnp.broadcast_to(m, full)
            m_new.append(m)
            probs.append(p.astype(BF16))
        out = []
        for i, p in enumerate(probs):
            pv = jnp.dot(p, jnp.concatenate([blocks[i][2], ones], axis=1),
                         preferred_element_type=F32)
            acc, l = pv[:, :LANES], pv[:, LANES:]
            if olds is not None:
                acc = rescale[i] * olds[i][2] + acc
                l = rescale[i] * olds[i][1] + l
            out.append((m_new[i], l, acc))
        return out

    for j0 in range(0, A_TILE // BLOCK, GROUP):
        blocks = []
        for j in range(j0, j0 + GROUP):
            r0 = j * NEAR_SLAB
            h0 = BLOCK + r0 - NEAR_SLAB
            blocks.append((
                jnp.concatenate([q4_ref[c, r0:r0 + NEAR_SLAB, :] for c in range(MID_DIL)], axis=0),
                jnp.concatenate([k4buf[c, h0:h0 + 2 * NEAR_SLAB, :] for c in range(MID_DIL)], axis=0),
                jnp.concatenate([v4buf[c, h0:h0 + 2 * NEAR_SLAB, :] for c in range(MID_DIL)], axis=0),
                near_first if j == 0 else near))
        for j, (m, l, acc) in zip(range(j0, j0 + GROUP), fold_group(blocks, None)):
            for c in range(MID_DIL):
                dst = slice(c * A_MID_ROWS + j * NEAR_SLAB, c * A_MID_ROWS + (j + 1) * NEAR_SLAB)
                src = slice(c * NEAR_SLAB, (c + 1) * NEAR_SLAB)
                acc_n[dst, :] = acc[src]
                m_n[dst, :] = m[src]
                l_n[dst, :] = l[src]

    for c in range(MID_DIL):
        for j0 in range(0, A_MID_ROWS // BLOCK, GROUP):
            rows = [slice(c * A_MID_ROWS + j * BLOCK, c * A_MID_ROWS + (j + 1) * BLOCK)
                    for j in range(j0, j0 + GROUP)]
            blocks = [(q4_ref[c, j * BLOCK:(j + 1) * BLOCK, :], k4buf[c, j * BLOCK:(j + 2) * BLOCK, :],
                       v4buf[c, j * BLOCK:(j + 2) * BLOCK, :], band_first if j == 0 else band)
                      for j in range(j0, j0 + GROUP)]
            olds = [(m_n[r, :], l_n[r, :], acc_n[r, :]) for r in rows]
            for r, (m, l, acc) in zip(rows, fold_group(blocks, olds)):
                acc_m[r, :] = acc
                m_m[r, :] = m
                l_m[r, :] = l

    for c in range(MID_DIL):
        for g0 in range(0, FAR_DIL // MID_DIL, GROUP):
            classes = [c + MID_DIL * c2 for c2 in range(g0, g0 + GROUP)]
            rows = [pl.ds(c * A_MID_ROWS + c2, BLOCK, stride=MID_DIL) for c2 in range(g0, g0 + GROUP)]
            blocks = [(q16_ref[r], k16buf[r], v16buf[r], band_first) for r in classes]
            olds = [(m_m[r, :], l_m[r, :], acc_m[r, :]) for r in rows]
            for r, (_, l, acc) in zip(rows, fold_group(blocks, olds)):
                o_s[r, :] = acc * (1.0 / l)

    for c in range(MID_DIL):
        y_s[pl.ds(c, A_MID_ROWS, stride=MID_DIL), :] = o_s[c * A_MID_ROWS:(c + 1) * A_MID_ROWS, :]
    y_ref[...] = y_s[...].astype(BF16)


def _attn_a_call(qkv4, qkv16):
    seq = qkv4.shape[0] * qkv4.shape[1]
    tiles = seq // A_TILE
    mid_blocks = A_MID_ROWS // BLOCK

    def mid_tile(which):
        return pl.BlockSpec((MID_DIL, A_MID_ROWS, A_HEAD_DIM),
                            lambda t, h: (0, t, which * A_HEADS + h))

    def mid_halo(which):
        return pl.BlockSpec((MID_DIL, BLOCK, A_HEAD_DIM),
                            lambda t, h: (0, jnp.maximum(t * mid_blocks - 1, 0), which * A_HEADS + h))

    def far_tile(which):
        return pl.BlockSpec((FAR_DIL, BLOCK, A_HEAD_DIM), lambda t, h: (0, t, which * A_HEADS + h))

    def far_halo(which):
        return pl.BlockSpec((FAR_DIL, BLOCK, A_HEAD_DIM),
                            lambda t, h: (0, jnp.maximum(t - 1, 0), which * A_HEADS + h))

    slab = A_TILE * A_HEAD_DIM
    vmem = (2 * (3 * slab + 2 * slab // 4) * 2
            + 2 * 5 * slab * 2
            + 2 * slab * 2
            + 2 * (slab + slab // 4) * 2 + 2 * 2 * slab * 2
            + 8 * slab * 4)
    return pl.pallas_call(
        _attn_a_kernel,
        grid=(tiles, A_HEADS),
        in_specs=[mid_tile(0), mid_tile(1), mid_halo(1), mid_tile(2), mid_halo(2),
                  far_tile(0), far_tile(1), far_halo(1), far_tile(2), far_halo(2)],
        out_specs=pl.BlockSpec((A_TILE, A_HEAD_DIM), lambda t, h: (t, h)),
        out_shape=jax.ShapeDtypeStruct((seq, A_WIDTH), BF16),
        scratch_shapes=[
            pltpu.VMEM((MID_DIL, A_MID_ROWS + BLOCK, A_HEAD_DIM), BF16),
            pltpu.VMEM((MID_DIL, A_MID_ROWS + BLOCK, A_HEAD_DIM), BF16),
            pltpu.VMEM((FAR_DIL, 2 * BLOCK, A_HEAD_DIM), BF16),
            pltpu.VMEM((FAR_DIL, 2 * BLOCK, A_HEAD_DIM), BF16),
        ] + [pltpu.VMEM((A_TILE, LANES), F32)] * 8,
        compiler_params=_params(("arbitrary", "arbitrary"), vmem),
        name="attn_a",
    )(qkv4, qkv4, qkv4, qkv4, qkv4, qkv16, qkv16, qkv16, qkv16, qkv16)


def _attn_b_kernel(sinks_ref, q_ref, km_ref, kh_ref, vm_ref, vh_ref, gain_ref, y_ref,
                   klo, khi, vlo, vhi, obuf, *, rows):
    lane = lax.broadcasted_iota(jnp.int32, (rows + BLOCK, LANES), 1)
    low = lane < B_HEAD_DIM

    def fill(dst_lo, dst_hi, halo_ref, main_ref):
        full = jnp.concatenate([halo_ref[...], main_ref[...]], axis=0).astype(F32)
        swapped = pltpu.roll(full, B_HEAD_DIM, 1)
        zero = jnp.zeros_like(full)
        for c in range(B_KV_HEADS):
            own_low, own_high = (full, swapped) if c == 0 else (swapped, full)
            dst_lo[c] = jnp.where(low, own_low, zero).astype(BF16)
            dst_hi[c] = jnp.where(low, zero, own_high).astype(BF16)

    fill(klo, khi, kh_ref, km_ref)
    fill(vlo, vhi, vh_ref, vm_ref)

    band, band_first = _band_masks(1, pl.program_id(0) == 0)

    for j in range(rows // BLOCK):
        r0 = j * BLOCK
        valid = band_first if j == 0 else band
        sum_sq = jnp.zeros((BLOCK, 1), F32)
        for pair in range(B_HEADS // 2):
            c = (2 * pair) // B_GROUP
            c0 = pair * LANES
            q = q_ref[r0:r0 + BLOCK, c0:c0 + LANES]
            o = jnp.zeros((BLOCK, LANES), F32)
            for odd, (k_sel, v_sel) in enumerate(((klo, vlo), (khi, vhi))):
                s = _scores(q, k_sel[c, r0:r0 + 2 * BLOCK, :], valid)
                m = jnp.max(s, axis=1, keepdims=True)
                p = jnp.exp(s - m)
                l = jnp.sum(p, axis=1, keepdims=True)
                acc = jnp.dot(p.astype(BF16), v_sel[c, r0:r0 + 2 * BLOCK, :],
                              preferred_element_type=F32)
                sink = sinks_ref[2 * pair + odd]
                m_all = jnp.maximum(m, sink)
                e = jnp.exp(m - m_all)
                denom = l * e + jnp.exp(sink - m_all)
                o = o + acc * (e / denom)
            obuf[r0:r0 + BLOCK, c0:c0 + LANES] = o
            sum_sq = sum_sq + jnp.sum(o * o, axis=1, keepdims=True)
        scale = _rms_scale(sum_sq, B_WIDTH)
        y_ref[r0:r0 + BLOCK, :] = (obuf[r0:r0 + BLOCK, :] * scale * gain_ref[...]).astype(BF16)


def _attn_b_call(qkv_b, sinks, gain):
    seq = qkv_b.shape[0]
    rows = B_ROWS
    blocks_per_tile = rows // BLOCK
    k_col = B_WIDTH // B_KV_WIDTH
    v_col = k_col + 1

    def kv_tile(col):
        return pl.BlockSpec((rows, B_KV_WIDTH), lambda t: (t, col))

    def kv_halo(col):
        return pl.BlockSpec((BLOCK, B_KV_WIDTH),
                            lambda t: (jnp.maximum(t * blocks_per_tile - 1, 0), col))

    kv_scratch = pltpu.VMEM((B_KV_HEADS, rows + BLOCK, LANES), BF16)
    vmem = (2 * 2 * rows * B_WIDTH * 2 + 4 * 2 * (rows + BLOCK) * LANES * 2
            + 4 * B_KV_HEADS * (rows + BLOCK) * LANES * 2 + rows * B_WIDTH * 4
            + 4 * (rows + BLOCK) * LANES * 4)
    return pl.pallas_call(
        functools.partial(_attn_b_kernel, rows=rows),
        grid=(seq // rows,),
        in_specs=[
            pl.BlockSpec(memory_space=pltpu.SMEM),
            pl.BlockSpec((rows, B_WIDTH), lambda t: (t, 0)),
            kv_tile(k_col), kv_halo(k_col), kv_tile(v_col), kv_halo(v_col),
            pl.BlockSpec((1, B_WIDTH), lambda t: (0, 0)),
        ],
        out_specs=pl.BlockSpec((rows, B_WIDTH), lambda t: (t, 0)),
        out_shape=jax.ShapeDtypeStruct((seq, B_WIDTH), BF16),
        scratch_shapes=[kv_scratch, kv_scratch, kv_scratch, kv_scratch,
                        pltpu.VMEM((rows, B_WIDTH), F32)],
        compiler_params=_params(("arbitrary",), vmem),
        name="attn_b",
    )(sinks, qkv_b, qkv_b, qkv_b, qkv_b, qkv_b, gain.reshape(1, B_WIDTH))


def _layer_norm_rows(z, g, b):
    mu = jnp.mean(z, axis=-1, keepdims=True)
    zc = z - mu
    var = jnp.mean(zc * zc, axis=-1, keepdims=True)
    return zc * lax.rsqrt(var + LN_EPS) * g + b


def _out_kernel(ya_ref, yb_ref, ga_ref, w_ref, x_ref, g_ref, b_ref, h_ref):
    ya = ya_ref[...].astype(F32)
    scale = _rms_scale(jnp.sum(ya * ya, axis=1, keepdims=True), A_WIDTH)
    ya = (ya * scale * ga_ref[...]).astype(BF16)
    mix = jnp.dot(ya, w_ref[0:A_WIDTH, :], preferred_element_type=F32)
    mix = mix + jnp.dot(yb_ref[...], w_ref[A_WIDTH:, :], preferred_element_type=F32)
    h_ref[...] = _layer_norm_rows(ALPHA * x_ref[...] + mix, g_ref[...], b_ref[...])


def _out_call(ya, yb, gain_a, w_out, x2d, ln_g, ln_b):
    seq = x2d.shape[0]
    tm = OUT_ROWS
    row_spec = pl.BlockSpec((1, D_MODEL), lambda i: (0, 0))
    vmem = (D_MODEL * D_MODEL * 2 + 2 * 2 * tm * A_WIDTH * 2
            + 2 * 2 * tm * D_MODEL * 4 + 2 * tm * D_MODEL * 4)
    return pl.pallas_call(
        _out_kernel,
        grid=(seq // tm,),
        in_specs=[
            pl.BlockSpec((tm, A_WIDTH), lambda i: (i, 0)),
            pl.BlockSpec((tm, B_WIDTH), lambda i: (i, 0)),
            pl.BlockSpec((1, A_WIDTH), lambda i: (0, 0)),
            pl.BlockSpec((D_MODEL, D_MODEL), lambda i: (0, 0), pipeline_mode=pl.Buffered(1)),
            pl.BlockSpec((tm, D_MODEL), lambda i: (i, 0)),
            row_spec, row_spec,
        ],
        out_specs=pl.BlockSpec((tm, D_MODEL), lambda i: (i, 0)),
        out_shape=jax.ShapeDtypeStruct((seq, D_MODEL), F32),
        compiler_params=_params(("arbitrary",), vmem),
        name="out_proj_ln",
    )(ya, yb, gain_a.reshape(1, A_WIDTH), w_out.astype(BF16), x2d,
      ln_g.reshape(1, D_MODEL), ln_b.reshape(1, D_MODEL))


def _ffn_kernel(h_ref, wg_ref, wu_ref, wd_ref, g_ref, b_ref, o_ref, hb_ref):
    j = pl.program_id(1)

    @pl.when(j == 0)
    def _():
        h = h_ref[...]
        hb_ref[...] = h.astype(BF16)
        o_ref[...] = ALPHA * h

    hb = hb_ref[...]
    gate = jnp.dot(hb, wg_ref[...], preferred_element_type=F32)
    up = jnp.dot(hb, wu_ref[...], preferred_element_type=F32)
    act = (gate * jax.nn.sigmoid(gate) * up).astype(BF16)
    o_ref[...] += jnp.dot(act, wd_ref[...], preferred_element_type=F32)

    @pl.when(j == pl.num_programs(1) - 1)
    def _():
        o_ref[...] = _layer_norm_rows(o_ref[...], g_ref[...], b_ref[...])


def _ffn_call(h, w_up, w_down, ln_g, ln_b):
    seq = h.shape[0]
    tm, tf = FFN_ROWS, FFN_CHUNK
    n_chunks = D_FF // tf
    w_up_b = w_up.astype(BF16)
    row_spec = pl.BlockSpec((1, D_MODEL), lambda i, j: (0, 0))
    vmem = (2 * tm * D_MODEL * 4 + tm * D_MODEL * 2 + 2 * tm * D_MODEL * 4
            + 2 * 3 * D_MODEL * tf * 2 + 3 * tm * tf * 4)
    return pl.pallas_call(
        _ffn_kernel,
        grid=(seq // tm, n_chunks),
        in_specs=[
            pl.BlockSpec((tm, D_MODEL), lambda i, j: (i, 0)),
            pl.BlockSpec((D_MODEL, tf), lambda i, j: (0, j)),
            pl.BlockSpec((D_MODEL, tf), lambda i, j: (0, n_chunks + j)),
            pl.BlockSpec((tf, D_MODEL), lambda i, j: (j, 0)),
            row_spec, row_spec,
        ],
        out_specs=pl.BlockSpec((tm, D_MODEL), lambda i, j: (i, 0)),
        out_shape=jax.ShapeDtypeStruct((seq, D_MODEL), F32),
        scratch_shapes=[pltpu.VMEM((tm, D_MODEL), BF16)],
        compiler_params=_params(("arbitrary", "arbitrary"), vmem),
        name="swiglu_ln",
    )(h, w_up_b, w_up_b, w_down.astype(BF16), ln_g.reshape(1, D_MODEL), ln_b.reshape(1, D_MODEL))


def _layer(x2d, w_in, b_in, sinks, g_mix_a, g_mix_b, w_out, ln1_g, ln1_b, w_up, w_down, ln2_g, ln2_b):
    qkv4, qkv16, qkv_b = _proj_call(x2d, w_in, b_in)
    ya = _attn_a_call(qkv4, qkv16)
    yb = _attn_b_call(qkv_b, sinks, g_mix_b)
    h = _out_call(ya, yb, g_mix_a, w_out, x2d, ln1_g, ln1_b)
    return _ffn_call(h, w_up, w_down, ln2_g, ln2_b)


def kernel(x, w_in, b_in, sinks, g_mix_a, g_mix_b, w_out, ln1_g, ln1_b, w_up, w_down, ln2_g, ln2_b):
    batch, seq, d_model = x.shape
    assert (batch, d_model) == (1, D_MODEL) and seq % A_TILE == 0
    h = x.reshape(seq, d_model)
    for i in range(w_in.shape[0]):
        h = _layer(h, w_in[i], b_in[i], sinks[i], g_mix_a[i], g_mix_b[i], w_out[i],
                   ln1_g[i], ln1_b[i], w_up[i], w_down[i], ln2_g[i], ln2_b[i])
    return h.reshape(batch, seq, d_model)
```

```python
import functools
import math

import jax
import jax.numpy as jnp
from jax import lax
from jax.experimental import pallas as pl
from jax.experimental.pallas import tpu as pltpu

F32 = jnp.float32
BF16 = jnp.bfloat16

D_MODEL = 2048
A_HEAD_DIM = 128
A_WIDTH = 1024
A_HEADS = 8
MID_DIL = 4
FAR_DIL = 16
B_HEAD_DIM = 64
B_WIDTH = 1024
B_HEADS = 16
B_KV_HEADS = 2
B_GROUP = 8
B_KV_WIDTH = 128
QKV_A_WIDTH = 3 * A_WIDTH
QKV_B_WIDTH = B_WIDTH + 2 * B_KV_WIDTH
IN_WIDTH = QKV_A_WIDTH + QKV_B_WIDTH
D_FF = 5632
BLOCK = 128
ROPE_THETA = 10000.0
ALPHA = 2.0 ** 0.25
LN_EPS = 1e-5
RMS_EPS = 1e-6

LANES = 128
V7X_VMEM_BYTES = 64 * 1024 * 1024
V7X_VMEM_REQUEST_CAP = V7X_VMEM_BYTES - 2 * 1024 * 1024

PROJ_ROWS = 512
PROJ_CHUNK = 512
A_TILE = FAR_DIL * BLOCK
B_ROWS = 512
OUT_ROWS = 512
OUT_SUB_ROWS = 256
FFN_ROWS = 1024
FFN_CHUNK = 512

A_MID_ROWS = A_TILE // MID_DIL
NEAR_SLAB = BLOCK // MID_DIL
GROUP = 4


def _vmem_limit(estimate_bytes):
    return int(min(V7X_VMEM_REQUEST_CAP, estimate_bytes + estimate_bytes // 4))


def _params(semantics, vmem_estimate):
    return pltpu.CompilerParams(dimension_semantics=semantics,
                                vmem_limit_bytes=_vmem_limit(vmem_estimate))


def _rope_factors(seq, tile, head_dim):
    half = head_dim // 2
    inv_freq = ROPE_THETA ** (-jnp.arange(half, dtype=F32) / half)
    reps = LANES // head_dim

    def expand(pos):
        ang = pos.astype(F32)[:, None] * inv_freq[None, :]
        cos = jnp.cos(ang)
        sin = jnp.sin(ang)
        return (jnp.tile(jnp.concatenate([cos, cos], axis=-1), (1, reps)),
                jnp.tile(jnp.concatenate([sin, sin], axis=-1), (1, reps)))

    return expand(jnp.arange(tile)) + expand(jnp.arange(seq // tile) * tile)


def _proj_kernel(x_ref, w_ref, b_ref, oca_ref, osa_ref, tca_ref, tsa_ref,
                 ocb_ref, osb_ref, tcb_ref, tsb_ref, wout_ref,
                 o4_ref, o16_ref, ob_ref, wout_bf_ref, nat_buf, mid_buf):
    rows = x_ref.shape[0]
    step = pl.program_id(0)
    wout_bf_ref[...] = wout_ref[...].astype(BF16)
    xb = x_ref[...].astype(BF16)
    lane = lax.broadcasted_iota(jnp.int32, (rows, LANES), 1)
    first_half_a = lane < (A_HEAD_DIM // 2)
    first_half_b = (lane % B_HEAD_DIM) < (B_HEAD_DIM // 2)

    def tables(off_c_ref, off_s_ref, start_c_ref, start_s_ref, first_half):
        oc, os_ = off_c_ref[...], off_s_ref[...]
        tc, ts = start_c_ref[pl.ds(step, 1), :], start_s_ref[pl.ds(step, 1), :]
        sin = ts * oc + tc * os_
        return tc * oc - ts * os_, jnp.where(first_half, -sin, sin)

    ca, sa = tables(oca_ref, osa_ref, tca_ref, tsa_ref, first_half_a)
    cb, sb = tables(ocb_ref, osb_ref, tcb_ref, tsb_ref, first_half_b)
    scale_a = 1.0 / math.sqrt(A_HEAD_DIM)
    scale_b = 1.0 / math.sqrt(B_HEAD_DIM)
    mid_rows = rows // MID_DIL
    far_rows = rows // FAR_DIL

    def rope_a(y):
        return y * ca + pltpu.roll(y, A_HEAD_DIM // 2, 1) * sa

    def rope_b(y):
        partner = jnp.where(first_half_b,
                            pltpu.roll(y, LANES - B_HEAD_DIM // 2, 1),
                            pltpu.roll(y, B_HEAD_DIM // 2, 1))
        return y * cb + partner * sb

    for c0 in range(0, IN_WIDTH, PROJ_CHUNK):
        c1 = min(c0 + PROJ_CHUNK, IN_WIDTH)
        acc = jnp.dot(xb, w_ref[:, c0:c1], preferred_element_type=F32) + b_ref[:, c0:c1]
        for g, col in enumerate(range(c0, c1, LANES)):
            y = acc[:, col - c0:col - c0 + LANES]
            if col < A_WIDTH:
                y = rope_a(y) * scale_a
            elif col < 2 * A_WIDTH:
                y = rope_a(y)
            elif col < QKV_A_WIDTH:
                pass
            elif col < QKV_A_WIDTH + B_WIDTH:
                y = rope_b(y) * scale_b
            elif col < QKV_A_WIDTH + B_WIDTH + B_KV_WIDTH:
                y = rope_b(y)
            if col >= QKV_A_WIDTH:
                ob_ref[:, col - QKV_A_WIDTH:col - QKV_A_WIDTH + LANES] = y.astype(BF16)
                continue
            nat_buf[g] = y
            for c in range(MID_DIL):
                cls = nat_buf[g, pl.ds(c, mid_rows, stride=MID_DIL), :]
                o4_ref[c, :, col:col + LANES] = cls.astype(BF16)
                mid_buf[g, c * mid_rows:(c + 1) * mid_rows, :] = cls
            for c in range(MID_DIL):
                for c2 in range(FAR_DIL // MID_DIL):
                    cls = mid_buf[g, pl.ds(c * mid_rows + c2, far_rows, stride=MID_DIL), :]
                    o16_ref[c + MID_DIL * c2, :, col:col + LANES] = cls.astype(BF16)


def _proj_call(x2d, w_in, b_in, w_out):
    seq = x2d.shape[0]
    tm = PROJ_ROWS
    steps = seq // tm
    rope_a = _rope_factors(seq, tm, A_HEAD_DIM)
    rope_b = _rope_factors(seq, tm, B_HEAD_DIM)
    offset_spec = pl.BlockSpec((tm, LANES), lambda i: (0, 0))
    start_spec = pl.BlockSpec((steps, LANES), lambda i: (0, 0))
    rope_specs = [offset_spec, offset_spec, start_spec, start_spec]
    wout_rows = D_MODEL // steps
    groups = PROJ_CHUNK // LANES
    vmem = (D_MODEL * IN_WIDTH * 2
            + 2 * tm * D_MODEL * 4
            + 2 * tm * (2 * QKV_A_WIDTH + QKV_B_WIDTH) * 2
            + 2 * 4 * (tm + steps) * LANES * 4
            + 2 * wout_rows * D_MODEL * 6
            + 2 * groups * tm * LANES * 4
            + tm * D_MODEL * 2 + 2 * tm * PROJ_CHUNK * 4)
    return pl.pallas_call(
        _proj_kernel,
        grid=(steps,),
        in_specs=[
            pl.BlockSpec((tm, D_MODEL), lambda i: (i, 0)),
            pl.BlockSpec((D_MODEL, IN_WIDTH), lambda i: (0, 0), pipeline_mode=pl.Buffered(1)),
            pl.BlockSpec((1, IN_WIDTH), lambda i: (0, 0)),
            *rope_specs, *rope_specs,
            pl.BlockSpec((wout_rows, D_MODEL), lambda i: (i, 0)),
        ],
        out_specs=[
            pl.BlockSpec((MID_DIL, tm // MID_DIL, QKV_A_WIDTH), lambda i: (0, i, 0)),
            pl.BlockSpec((FAR_DIL, tm // FAR_DIL, QKV_A_WIDTH), lambda i: (0, i, 0)),
            pl.BlockSpec((tm, QKV_B_WIDTH), lambda i: (i, 0)),
            pl.BlockSpec((wout_rows, D_MODEL), lambda i: (i, 0)),
        ],
        out_shape=[
            jax.ShapeDtypeStruct((MID_DIL, seq // MID_DIL, QKV_A_WIDTH), BF16),
            jax.ShapeDtypeStruct((FAR_DIL, seq // FAR_DIL, QKV_A_WIDTH), BF16),
            jax.ShapeDtypeStruct((seq, QKV_B_WIDTH), BF16),
            jax.ShapeDtypeStruct((D_MODEL, D_MODEL), BF16),
        ],
        scratch_shapes=[pltpu.VMEM((groups, tm, LANES), F32), pltpu.VMEM((groups, tm, LANES), F32)],
        compiler_params=_params(("arbitrary",), vmem),
        name="proj_rope",
    )(x2d, w_in.astype(BF16), b_in.reshape(1, IN_WIDTH), *rope_a, *rope_b, w_out)


def _band_masks(lo, is_first_tile):
    qi = lax.broadcasted_iota(jnp.int32, (BLOCK, 2 * BLOCK), 0)
    kj = lax.broadcasted_iota(jnp.int32, (BLOCK, 2 * BLOCK), 1)
    band = (kj >= qi + lo) & (kj <= qi + BLOCK)
    first_lower = jnp.where(is_first_tile, BLOCK, 0)
    return band, band & (kj >= first_lower)


def _near_masks(is_first_tile):
    iq = lax.broadcasted_iota(jnp.int32, (BLOCK, 2 * BLOCK), 0)
    ik = lax.broadcasted_iota(jnp.int32, (BLOCK, 2 * BLOCK), 1)
    cq, mq = iq // NEAR_SLAB, iq % NEAR_SLAB
    ck, mk = ik // (2 * NEAR_SLAB), ik % (2 * NEAR_SLAB)
    dist = MID_DIL * (mq + NEAR_SLAB - mk) + (cq - ck)
    valid = (dist >= 0) & (dist <= BLOCK)
    first_lower = jnp.where(is_first_tile, NEAR_SLAB, 0)
    return valid, valid & (mk >= first_lower)


def _rms_scale(sum_sq, width):
    return lax.rsqrt(sum_sq * (1.0 / width) + RMS_EPS)


def _scores(q, k, valid):
    s = lax.dot_general(q, k, (((1,), (1,)), ((), ())), preferred_element_type=F32)
    return jnp.where(valid, s, -jnp.inf)


def _attn_a_kernel(q4_ref, k4_ref, k4h_ref, v4_ref, v4h_ref,
                   q16_ref, k16_ref, k16h_ref, v16_ref, v16h_ref, y_ref,
                   k4buf, v4buf, k16buf, v16buf,
                   acc_n, m_n, l_n, acc_m, m_m, l_m, o_s, y_s):
    first = pl.program_id(0) == 0
    k4buf[:, 0:BLOCK, :] = k4h_ref[...]
    k4buf[:, BLOCK:, :] = k4_ref[...]
    v4buf[:, 0:BLOCK, :] = v4h_ref[...]
    v4buf[:, BLOCK:, :] = v4_ref[...]
    k16buf[:, 0:BLOCK, :] = k16h_ref[...]
    k16buf[:, BLOCK:, :] = k16_ref[...]
    v16buf[:, 0:BLOCK, :] = v16h_ref[...]
    v16buf[:, BLOCK:, :] = v16_ref[...]

    near, near_first = _near_masks(first)
    band, band_first = _band_masks(0, first)
    full = (BLOCK, LANES)
    ones = jnp.ones((2 * BLOCK, LANES), BF16)

    def fold_group(blocks, olds):
        scores = [_scores(q, k, valid) for q, k, _, valid in blocks]
        m_new, rescale, probs = [], [], []
        for i, s in enumerate(scores):
            m = jnp.max(s, axis=1, keepdims=True)
            if olds is not None:
                m = jnp.maximum(olds[i][0], m)
                rescale.append(jnp.exp(olds[i][0] - m))
                p = jnp.concatenate([jnp.exp(s[:, :LANES] - m), jnp.exp(s[:, LANES:] - m)], axis=1)
            else:
                p = jnp.exp(s - m)
                m = jnp.broadcast_to(m, full)
            m_new.append(m)
            probs.append(p.astype(BF16))
        out = []
        for i, p in enumerate(probs):
            pv = jnp.dot(p, jnp.concatenate([blocks[i][2], ones], axis=1),
                         preferred_element_type=F32)
            acc, l = pv[:, :LANES], pv[:, LANES:]
            if olds is not None:
                acc = rescale[i] * olds[i][2] + acc
                l = rescale[i] * olds[i][1] + l
            out.append((m_new[i], l, acc))
        return out

    for j0 in range(0, A_TILE // BLOCK, GROUP):
        blocks = []
        for j in range(j0, j0 + GROUP):
            r0 = j * NEAR_SLAB
            h0 = BLOCK + r0 - NEAR_SLAB
            blocks.append((
                jnp.concatenate([q4_ref[c, r0:r0 + NEAR_SLAB, :] for c in range(MID_DIL)], axis=0),
                jnp.concatenate([k4buf[c, h0:h0 + 2 * NEAR_SLAB, :] for c in range(MID_DIL)], axis=0),
                jnp.concatenate([v4buf[c, h0:h0 + 2 * NEAR_SLAB, :] for c in range(MID_DIL)], axis=0),
                near_first if j == 0 else near))
        for j, (m, l, acc) in zip(range(j0, j0 + GROUP), fold_group(blocks, None)):
            for c in range(MID_DIL):
                dst = slice(c * A_MID_ROWS + j * NEAR_SLAB, c * A_MID_ROWS + (j + 1) * NEAR_SLAB)
                src = slice(c * NEAR_SLAB, (c + 1) * NEAR_SLAB)
                acc_n[dst, :] = acc[src]
                m_n[dst, :] = m[src]
                l_n[dst, :] = l[src]

    for c in range(MID_DIL):
        for j0 in range(0, A_MID_ROWS // BLOCK, GROUP):
            rows = [slice(c * A_MID_ROWS + j * BLOCK, c * A_MID_ROWS + (j + 1) * BLOCK)
                    for j in range(j0, j0 + GROUP)]
            blocks = [(q4_ref[c, j * BLOCK:(j + 1) * BLOCK, :], k4buf[c, j * BLOCK:(j + 2) * BLOCK, :],
                       v4buf[c, j * BLOCK:(j + 2) * BLOCK, :], band_first if j == 0 else band)
                      for j in range(j0, j0 + GROUP)]
            olds = [(m_n[r, :], l_n[r, :], acc_n[r, :]) for r in rows]
            for r, (m, l, acc) in zip(rows, fold_group(blocks, olds)):
                acc_m[r, :] = acc
                m_m[r, :] = m
                l_m[r, :] = l

    for c in range(MID_DIL):
        for g0 in range(0, FAR_DIL // MID_DIL, GROUP):
            classes = [c + MID_DIL * c2 for c2 in range(g0, g0 + GROUP)]
            rows = [pl.ds(c * A_MID_ROWS + c2, BLOCK, stride=MID_DIL) for c2 in range(g0, g0 + GROUP)]
            blocks = [(q16_ref[r], k16buf[r], v16buf[r], band_first) for r in classes]
            olds = [(m_m[r, :], l_m[r, :], acc_m[r, :]) for r in rows]
            for r, (_, l, acc) in zip(rows, fold_group(blocks, olds)):
                o_s[r, :] = acc * (1.0 / l)

    for c in range(MID_DIL):
        y_s[pl.ds(c, A_MID_ROWS, stride=MID_DIL), :] = o_s[c * A_MID_ROWS:(c + 1) * A_MID_ROWS, :]
    y_ref[...] = y_s[...].astype(BF16)


def _attn_a_call(qkv4, qkv16):
    seq = qkv4.shape[0] * qkv4.shape[1]
    tiles = seq // A_TILE
    mid_blocks = A_MID_ROWS // BLOCK

    def mid_tile(which):
        return pl.BlockSpec((MID_DIL, A_MID_ROWS, A_HEAD_DIM),
                            lambda t, h: (0, t, which * A_HEADS + h))

    def mid_halo(which):
        return pl.BlockSpec((MID_DIL, BLOCK, A_HEAD_DIM),
                            lambda t, h: (0, jnp.maximum(t * mid_blocks - 1, 0), which * A_HEADS + h))

    def far_tile(which):
        return pl.BlockSpec((FAR_DIL, BLOCK, A_HEAD_DIM), lambda t, h: (0, t, which * A_HEADS + h))

    def far_halo(which):
        return pl.BlockSpec((FAR_DIL, BLOCK, A_HEAD_DIM),
                            lambda t, h: (0, jnp.maximum(t - 1, 0), which * A_HEADS + h))

    slab = A_TILE * A_HEAD_DIM
    vmem = (2 * (3 * slab + 2 * slab // 4) * 2
            + 2 * 5 * slab * 2
            + 2 * slab * 2
            + 2 * (slab + slab // 4) * 2 + 2 * 2 * slab * 2
            + 8 * slab * 4)
    return pl.pallas_call(
        _attn_a_kernel,
        grid=(tiles, A_HEADS),
        in_specs=[mid_tile(0), mid_tile(1), mid_halo(1), mid_tile(2), mid_halo(2),
                  far_tile(0), far_tile(1), far_halo(1), far_tile(2), far_halo(2)],
        out_specs=pl.BlockSpec((A_TILE, A_HEAD_DIM), lambda t, h: (t, h)),
        out_shape=jax.ShapeDtypeStruct((seq, A_WIDTH), BF16),
        scratch_shapes=[
            pltpu.VMEM((MID_DIL, A_MID_ROWS + BLOCK, A_HEAD_DIM), BF16),
            pltpu.VMEM((MID_DIL, A_MID_ROWS + BLOCK, A_HEAD_DIM), BF16),
            pltpu.VMEM((FAR_DIL, 2 * BLOCK, A_HEAD_DIM), BF16),
            pltpu.VMEM((FAR_DIL, 2 * BLOCK, A_HEAD_DIM), BF16),
        ] + [pltpu.VMEM((A_TILE, LANES), F32)] * 8,
        compiler_params=_params(("arbitrary", "arbitrary"), vmem),
        name="attn_a",
    )(qkv4, qkv4, qkv4, qkv4, qkv4, qkv16, qkv16, qkv16, qkv16, qkv16)


def _attn_b_kernel(sinks_ref, q_ref, km_ref, kh_ref, vm_ref, vh_ref, gain_ref, y_ref,
                   klo, khi, vlo, vhi, obuf, *, rows):
    lane = lax.broadcasted_iota(jnp.int32, (rows + BLOCK, LANES), 1)
    low = lane < B_HEAD_DIM

    def fill(dst_lo, dst_hi, halo_ref, main_ref):
        full = jnp.concatenate([halo_ref[...], main_ref[...]], axis=0).astype(F32)
        swapped = pltpu.roll(full, B_HEAD_DIM, 1)
        zero = jnp.zeros_like(full)
        for c in range(B_KV_HEADS):
            own_low, own_high = (full, swapped) if c == 0 else (swapped, full)
            dst_lo[c] = jnp.where(low, own_low, zero).astype(BF16)
            dst_hi[c] = jnp.where(low, zero, own_high).astype(BF16)

    fill(klo, khi, kh_ref, km_ref)
    fill(vlo, vhi, vh_ref, vm_ref)

    band, band_first = _band_masks(1, pl.program_id(0) == 0)

    for j in range(rows // BLOCK):
        r0 = j * BLOCK
        valid = band_first if j == 0 else band
        sum_sq = jnp.zeros((BLOCK, 1), F32)
        for pair in range(B_HEADS // 2):
            c = (2 * pair) // B_GROUP
            c0 = pair * LANES
            q = q_ref[r0:r0 + BLOCK, c0:c0 + LANES]
            o = jnp.zeros((BLOCK, LANES), F32)
            for odd, (k_sel, v_sel) in enumerate(((klo, vlo), (khi, vhi))):
                s = _scores(q, k_sel[c, r0:r0 + 2 * BLOCK, :], valid)
                m = jnp.max(s, axis=1, keepdims=True)
                p = jnp.exp(s - m)
                l = jnp.sum(p, axis=1, keepdims=True)
                acc = jnp.dot(p.astype(BF16), v_sel[c, r0:r0 + 2 * BLOCK, :],
                              preferred_element_type=F32)
                sink = sinks_ref[2 * pair + odd]
                m_all = jnp.maximum(m, sink)
                e = jnp.exp(m - m_all)
                denom = l * e + jnp.exp(sink - m_all)
                o = o + acc * (e / denom)
            obuf[r0:r0 + BLOCK, c0:c0 + LANES] = o
            sum_sq = sum_sq + jnp.sum(o * o, axis=1, keepdims=True)
        scale = _rms_scale(sum_sq, B_WIDTH)
        y_ref[r0:r0 + BLOCK, :] = (obuf[r0:r0 + BLOCK, :] * scale * gain_ref[...]).astype(BF16)


def _attn_b_call(qkv_b, sinks, gain):
    seq = qkv_b.shape[0]
    rows = B_ROWS
    blocks_per_tile = rows // BLOCK
    k_col = B_WIDTH // B_KV_WIDTH
    v_col = k_col + 1

    def kv_tile(col):
        return pl.BlockSpec((rows, B_KV_WIDTH), lambda t: (t, col))

    def kv_halo(col):
        return pl.BlockSpec((BLOCK, B_KV_WIDTH),
                            lambda t: (jnp.maximum(t * blocks_per_tile - 1, 0), col))

    kv_scratch = pltpu.VMEM((B_KV_HEADS, rows + BLOCK, LANES), BF16)
    vmem = (2 * 2 * rows * B_WIDTH * 2 + 4 * 2 * (rows + BLOCK) * LANES * 2
            + 4 * B_KV_HEADS * (rows + BLOCK) * LANES * 2 + rows * B_WIDTH * 4
            + 4 * (rows + BLOCK) * LANES * 4)
    return pl.pallas_call(
        functools.partial(_attn_b_kernel, rows=rows),
        grid=(seq // rows,),
        in_specs=[
            pl.BlockSpec(memory_space=pltpu.SMEM),
            pl.BlockSpec((rows, B_WIDTH), lambda t: (t, 0)),
            kv_tile(k_col), kv_halo(k_col), kv_tile(v_col), kv_halo(v_col),
            pl.BlockSpec((1, B_WIDTH), lambda t: (0, 0)),
        ],
        out_specs=pl.BlockSpec((rows, B_WIDTH), lambda t: (t, 0)),
        out_shape=jax.ShapeDtypeStruct((seq, B_WIDTH), BF16),
        scratch_shapes=[kv_scratch, kv_scratch, kv_scratch, kv_scratch,
                        pltpu.VMEM((rows, B_WIDTH), F32)],
        compiler_params=_params(("arbitrary",), vmem),
        name="attn_b",
    )(sinks, qkv_b, qkv_b, qkv_b, qkv_b, qkv_b, gain.reshape(1, B_WIDTH))


def _layer_norm_rows(z, g, b):
    mu = jnp.mean(z, axis=-1, keepdims=True)
    zc = z - mu
    var = jnp.mean(zc * zc, axis=-1, keepdims=True)
    return zc * lax.rsqrt(var + LN_EPS) * g + b


def _out_kernel(ya_ref, yb_ref, ga_ref, w_ref, x_ref, g_ref, b_ref, wup_ref, wdn_ref,
                h_ref, wup_bf_ref, wdn_bf_ref):
    wup_bf_ref[...] = wup_ref[...].astype(BF16)
    wdn_bf_ref[...] = wdn_ref[...].astype(BF16)
    blocks = [slice(r, r + OUT_SUB_ROWS) for r in range(0, ya_ref.shape[0], OUT_SUB_ROWS)]
    mixes = []
    for rows in blocks:
        ya = ya_ref[rows, :].astype(F32)
        scale = _rms_scale(jnp.sum(ya * ya, axis=1, keepdims=True), A_WIDTH)
        ya = (ya * scale * ga_ref[...]).astype(BF16)
        mix = jnp.dot(ya, w_ref[0:A_WIDTH, :], preferred_element_type=F32)
        mixes.append(mix + jnp.dot(yb_ref[rows, :], w_ref[A_WIDTH:, :], preferred_element_type=F32))
    for rows, mix in zip(blocks, mixes):
        h_ref[rows, :] = _layer_norm_rows(ALPHA * x_ref[rows, :] + mix, g_ref[...], b_ref[...])


def _out_call(ya, yb, gain_a, w_out_bf, x2d, ln_g, ln_b, w_up, w_down):
    seq = x2d.shape[0]
    tm = OUT_ROWS
    steps = seq // tm
    up_rows, down_rows = D_MODEL // steps, D_FF // steps
    row_spec = pl.BlockSpec((1, D_MODEL), lambda i: (0, 0))
    vmem = (D_MODEL * D_MODEL * 2 + 2 * 2 * tm * A_WIDTH * 2
            + 2 * 2 * tm * D_MODEL * 4 + 2 * tm * D_MODEL * 4
            + 2 * (up_rows * 2 * D_FF + down_rows * D_MODEL) * 6)
    return pl.pallas_call(
        _out_kernel,
        grid=(steps,),
        in_specs=[
            pl.BlockSpec((tm, A_WIDTH), lambda i: (i, 0)),
            pl.BlockSpec((tm, B_WIDTH), lambda i: (i, 0)),
            pl.BlockSpec((1, A_WIDTH), lambda i: (0, 0)),
            pl.BlockSpec((D_MODEL, D_MODEL), lambda i: (0, 0), pipeline_mode=pl.Buffered(1)),
            pl.BlockSpec((tm, D_MODEL), lambda i: (i, 0)),
            row_spec, row_spec,
            pl.BlockSpec((up_rows, 2 * D_FF), lambda i: (i, 0)),
            pl.BlockSpec((down_rows, D_MODEL), lambda i: (i, 0)),
        ],
        out_specs=[
            pl.BlockSpec((tm, D_MODEL), lambda i: (i, 0)),
            pl.BlockSpec((up_rows, 2 * D_FF), lambda i: (i, 0)),
            pl.BlockSpec((down_rows, D_MODEL), lambda i: (i, 0)),
        ],
        out_shape=[
            jax.ShapeDtypeStruct((seq, D_MODEL), F32),
            jax.ShapeDtypeStruct((D_MODEL, 2 * D_FF), BF16),
            jax.ShapeDtypeStruct((D_FF, D_MODEL), BF16),
        ],
        compiler_params=_params(("arbitrary",), vmem),
        name="out_proj_ln",
    )(ya, yb, gain_a.reshape(1, A_WIDTH), w_out_bf, x2d,
      ln_g.reshape(1, D_MODEL), ln_b.reshape(1, D_MODEL), w_up, w_down)


def _ffn_kernel(h_ref, wg_ref, wu_ref, wd_ref, g_ref, b_ref, o_ref, hb_ref):
    j = pl.program_id(1)

    @pl.when(j == 0)
    def _():
        h = h_ref[...]
        hb_ref[...] = h.astype(BF16)
        o_ref[...] = ALPHA * h

    hb = hb_ref[...]
    half = wg_ref.shape[1] // 2
    halves = [slice(0, half), slice(half, 2 * half)]
    gates = [jnp.dot(hb, wg_ref[:, s], preferred_element_type=F32) for s in halves]
    ups = [jnp.dot(hb, wu_ref[:, s], preferred_element_type=F32) for s in halves]
    acts = [(g * jax.nn.sigmoid(g) * u).astype(BF16) for g, u in zip(gates, ups)]
    down = [jnp.dot(a, wd_ref[s, :], preferred_element_type=F32) for a, s in zip(acts, halves)]
    o_ref[...] += down[0] + down[1]

    @pl.when(j == pl.num_programs(1) - 1)
    def _():
        o_ref[...] = _layer_norm_rows(o_ref[...], g_ref[...], b_ref[...])


def _ffn_call(h, w_up_b, w_down_b, ln_g, ln_b):
    seq = h.shape[0]
    tm, tf = FFN_ROWS, FFN_CHUNK
    n_chunks = D_FF // tf
    row_spec = pl.BlockSpec((1, D_MODEL), lambda i, j: (0, 0))
    vmem = (2 * tm * D_MODEL * 4 + tm * D_MODEL * 2 + 2 * tm * D_MODEL * 4
            + 2 * 3 * D_MODEL * tf * 2 + 3 * tm * tf * 4)
    return pl.pallas_call(
        _ffn_kernel,
        grid=(seq // tm, n_chunks),
        in_specs=[
            pl.BlockSpec((tm, D_MODEL), lambda i, j: (i, 0)),
            pl.BlockSpec((D_MODEL, tf), lambda i, j: (0, j)),
            pl.BlockSpec((D_MODEL, tf), lambda i, j: (0, n_chunks + j)),
            pl.BlockSpec((tf, D_MODEL), lambda i, j: (j, 0)),
            row_spec, row_spec,
        ],
        out_specs=pl.BlockSpec((tm, D_MODEL), lambda i, j: (i, 0)),
        out_shape=jax.ShapeDtypeStruct((seq, D_MODEL), F32),
        scratch_shapes=[pltpu.VMEM((tm, D_MODEL), BF16)],
        compiler_params=_params(("arbitrary", "arbitrary"), vmem),
        name="swiglu_ln",
    )(h, w_up_b, w_up_b, w_down_b, ln_g.reshape(1, D_MODEL), ln_b.reshape(1, D_MODEL))


def _layer(x2d, w_in, b_in, sinks, g_mix_a, g_mix_b, w_out, ln1_g, ln1_b, w_up, w_down, ln2_g, ln2_b):
    qkv4, qkv16, qkv_b, w_out_b = _proj_call(x2d, w_in, b_in, w_out)
    ya = _attn_a_call(qkv4, qkv16)
    yb = _attn_b_call(qkv_b, sinks, g_mix_b)
    h, w_up_b, w_down_b = _out_call(ya, yb, g_mix_a, w_out_b, x2d, ln1_g, ln1_b, w_up, w_down)
    return _ffn_call(h, w_up_b, w_down_b, ln2_g, ln2_b)


def kernel(x, w_in, b_in, sinks, g_mix_a, g_mix_b, w_out, ln1_g, ln1_b, w_up, w_down, ln2_g, ln2_b):
    batch, seq, d_model = x.shape
    assert (batch, d_model) == (1, D_MODEL) and seq % A_TILE == 0
    h = x.reshape(seq, d_model)
    for i in range(w_in.shape[0]):
        h = _layer(h, w_in[i], b_in[i], sinks[i], g_mix_a[i], g_mix_b[i], w_out[i],
                   ln1_g[i], ln1_b[i], w_up[i], w_down[i], ln2_g[i], ln2_b[i])
    return h.reshape(batch, seq, d_model)
```

```python
import functools
import math

import jax
import jax.numpy as jnp
from jax import lax
from jax.experimental import pallas as pl
from jax.experimental.pallas import tpu as pltpu

F32 = jnp.float32
BF16 = jnp.bfloat16

D_MODEL = 2048
A_HEAD_DIM = 128
A_WIDTH = 1024
A_HEADS = 8
MID_DIL = 4
FAR_DIL = 16
B_HEAD_DIM = 64
B_WIDTH = 1024
B_HEADS = 16
B_KV_HEADS = 2
B_GROUP = 8
B_KV_WIDTH = 128
QKV_A_WIDTH = 3 * A_WIDTH
QKV_B_WIDTH = B_WIDTH + 2 * B_KV_WIDTH
IN_WIDTH = QKV_A_WIDTH + QKV_B_WIDTH
D_FF = 5632
BLOCK = 128
ROPE_THETA = 10000.0
ALPHA = 2.0 ** 0.25
LN_EPS = 1e-5
RMS_EPS = 1e-6

LANES = 128
V7X_VMEM_BYTES = 64 * 1024 * 1024
V7X_VMEM_REQUEST_CAP = V7X_VMEM_BYTES - 2 * 1024 * 1024

PROJ_ROWS = 512
PROJ_CHUNK = 512
A_TILE = FAR_DIL * BLOCK
B_ROWS = 512
OUT_ROWS = 512
OUT_SUB_ROWS = 256
FFN_ROWS = 1024
FFN_CHUNK = 512

A_MID_ROWS = A_TILE // MID_DIL
NEAR_SLAB = BLOCK // MID_DIL
GROUP = 4
B_ITEM_GROUP = 2


def _vmem_limit(estimate_bytes):
    return int(min(V7X_VMEM_REQUEST_CAP, estimate_bytes + estimate_bytes // 4))


def _params(semantics, vmem_estimate):
    return pltpu.CompilerParams(dimension_semantics=semantics,
                                vmem_limit_bytes=_vmem_limit(vmem_estimate))


def _rope_factors(seq, tile, head_dim):
    half = head_dim // 2
    inv_freq = ROPE_THETA ** (-jnp.arange(half, dtype=F32) / half)
    reps = LANES // head_dim

    def expand(pos):
        ang = pos.astype(F32)[:, None] * inv_freq[None, :]
        cos = jnp.cos(ang)
        sin = jnp.sin(ang)
        return (jnp.tile(jnp.concatenate([cos, cos], axis=-1), (1, reps)),
                jnp.tile(jnp.concatenate([sin, sin], axis=-1), (1, reps)))

    return expand(jnp.arange(tile)) + expand(jnp.arange(seq // tile) * tile)


def _proj_kernel(x_ref, w_ref, b_ref, oca_ref, osa_ref, tca_ref, tsa_ref,
                 ocb_ref, osb_ref, tcb_ref, tsb_ref, wout_ref, wup_ref,
                 o4_ref, o16_ref, ob_ref, wout_bf_ref, wup_bf_ref, nat_buf, mid_buf):
    rows = x_ref.shape[0]
    step = pl.program_id(0)
    wout_bf_ref[...] = wout_ref[...].astype(BF16)
    wup_bf_ref[...] = wup_ref[...].astype(BF16)
    xb = x_ref[...].astype(BF16)
    lane = lax.broadcasted_iota(jnp.int32, (rows, LANES), 1)
    first_half_a = lane < (A_HEAD_DIM // 2)
    first_half_b = (lane % B_HEAD_DIM) < (B_HEAD_DIM // 2)

    def tables(off_c_ref, off_s_ref, start_c_ref, start_s_ref, first_half):
        oc, os_ = off_c_ref[...], off_s_ref[...]
        tc, ts = start_c_ref[pl.ds(step, 1), :], start_s_ref[pl.ds(step, 1), :]
        sin = ts * oc + tc * os_
        return tc * oc - ts * os_, jnp.where(first_half, -sin, sin)

    ca, sa = tables(oca_ref, osa_ref, tca_ref, tsa_ref, first_half_a)
    cb, sb = tables(ocb_ref, osb_ref, tcb_ref, tsb_ref, first_half_b)
    scale_a = 1.0 / math.sqrt(A_HEAD_DIM)
    scale_b = 1.0 / math.sqrt(B_HEAD_DIM)
    mid_rows = rows // MID_DIL
    far_rows = rows // FAR_DIL

    def rope_a(y):
        return y * ca + pltpu.roll(y, A_HEAD_DIM // 2, 1) * sa

    def rope_b(y):
        partner = jnp.where(first_half_b,
                            pltpu.roll(y, LANES - B_HEAD_DIM // 2, 1),
                            pltpu.roll(y, B_HEAD_DIM // 2, 1))
        return y * cb + partner * sb

    for c0 in range(0, IN_WIDTH, PROJ_CHUNK):
        c1 = min(c0 + PROJ_CHUNK, IN_WIDTH)
        acc = jnp.dot(xb, w_ref[:, c0:c1], preferred_element_type=F32) + b_ref[:, c0:c1]
        for g, col in enumerate(range(c0, c1, LANES)):
            y = acc[:, col - c0:col - c0 + LANES]
            if col < A_WIDTH:
                y = rope_a(y) * scale_a
            elif col < 2 * A_WIDTH:
                y = rope_a(y)
            elif col < QKV_A_WIDTH:
                pass
            elif col < QKV_A_WIDTH + B_WIDTH:
                y = rope_b(y) * scale_b
            elif col < QKV_A_WIDTH + B_WIDTH + B_KV_WIDTH:
                y = rope_b(y)
            if col >= QKV_A_WIDTH:
                ob_ref[:, col - QKV_A_WIDTH:col - QKV_A_WIDTH + LANES] = y.astype(BF16)
                continue
            nat_buf[g] = y
            for c in range(MID_DIL):
                cls = nat_buf[g, pl.ds(c, mid_rows, stride=MID_DIL), :]
                o4_ref[c, :, col:col + LANES] = cls.astype(BF16)
                mid_buf[g, c * mid_rows:(c + 1) * mid_rows, :] = cls
            for c in range(MID_DIL):
                for c2 in range(FAR_DIL // MID_DIL):
                    cls = mid_buf[g, pl.ds(c * mid_rows + c2, far_rows, stride=MID_DIL), :]
                    o16_ref[c + MID_DIL * c2, :, col:col + LANES] = cls.astype(BF16)


def _proj_call(x2d, w_in, b_in, w_out, w_up):
    seq = x2d.shape[0]
    tm = PROJ_ROWS
    steps = seq // tm
    rope_a = _rope_factors(seq, tm, A_HEAD_DIM)
    rope_b = _rope_factors(seq, tm, B_HEAD_DIM)
    offset_spec = pl.BlockSpec((tm, LANES), lambda i: (0, 0))
    start_spec = pl.BlockSpec((steps, LANES), lambda i: (0, 0))
    rope_specs = [offset_spec, offset_spec, start_spec, start_spec]
    wout_rows = D_MODEL // steps
    groups = PROJ_CHUNK // LANES
    vmem = (D_MODEL * IN_WIDTH * 2
            + 2 * tm * D_MODEL * 4
            + 2 * tm * (2 * QKV_A_WIDTH + QKV_B_WIDTH) * 2
            + 2 * 4 * (tm + steps) * LANES * 4
            + 2 * wout_rows * (D_MODEL + 2 * D_FF) * 6
            + 2 * groups * tm * LANES * 4
            + tm * D_MODEL * 2 + 2 * tm * PROJ_CHUNK * 4)
    return pl.pallas_call(
        _proj_kernel,
        grid=(steps,),
        in_specs=[
            pl.BlockSpec((tm, D_MODEL), lambda i: (i, 0)),
            pl.BlockSpec((D_MODEL, IN_WIDTH), lambda i: (0, 0), pipeline_mode=pl.Buffered(1)),
            pl.BlockSpec((1, IN_WIDTH), lambda i: (0, 0)),
            *rope_specs, *rope_specs,
            pl.BlockSpec((wout_rows, D_MODEL), lambda i: (i, 0)),
            pl.BlockSpec((wout_rows, 2 * D_FF), lambda i: (i, 0)),
        ],
        out_specs=[
            pl.BlockSpec((MID_DIL, tm // MID_DIL, QKV_A_WIDTH), lambda i: (0, i, 0)),
            pl.BlockSpec((FAR_DIL, tm // FAR_DIL, QKV_A_WIDTH), lambda i: (0, i, 0)),
            pl.BlockSpec((tm, QKV_B_WIDTH), lambda i: (i, 0)),
            pl.BlockSpec((wout_rows, D_MODEL), lambda i: (i, 0)),
            pl.BlockSpec((wout_rows, 2 * D_FF), lambda i: (i, 0)),
        ],
        out_shape=[
            jax.ShapeDtypeStruct((MID_DIL, seq // MID_DIL, QKV_A_WIDTH), BF16),
            jax.ShapeDtypeStruct((FAR_DIL, seq // FAR_DIL, QKV_A_WIDTH), BF16),
            jax.ShapeDtypeStruct((seq, QKV_B_WIDTH), BF16),
            jax.ShapeDtypeStruct((D_MODEL, D_MODEL), BF16),
            jax.ShapeDtypeStruct((D_MODEL, 2 * D_FF), BF16),
        ],
        scratch_shapes=[pltpu.VMEM((groups, tm, LANES), F32), pltpu.VMEM((groups, tm, LANES), F32)],
        compiler_params=_params(("arbitrary",), vmem),
        name="proj_rope",
    )(x2d, w_in.astype(BF16), b_in.reshape(1, IN_WIDTH), *rope_a, *rope_b, w_out, w_up)


def _band_masks(lo, is_first_tile):
    qi = lax.broadcasted_iota(jnp.int32, (BLOCK, 2 * BLOCK), 0)
    kj = lax.broadcasted_iota(jnp.int32, (BLOCK, 2 * BLOCK), 1)
    band = (kj >= qi + lo) & (kj <= qi + BLOCK)
    first_lower = jnp.where(is_first_tile, BLOCK, 0)
    return band, band & (kj >= first_lower)


def _near_masks(is_first_tile):
    iq = lax.broadcasted_iota(jnp.int32, (BLOCK, 2 * BLOCK), 0)
    ik = lax.broadcasted_iota(jnp.int32, (BLOCK, 2 * BLOCK), 1)
    cq, mq = iq // NEAR_SLAB, iq % NEAR_SLAB
    ck, mk = ik // (2 * NEAR_SLAB), ik % (2 * NEAR_SLAB)
    dist = MID_DIL * (mq + NEAR_SLAB - mk) + (cq - ck)
    valid = (dist >= 0) & (dist <= BLOCK)
    first_lower = jnp.where(is_first_tile, NEAR_SLAB, 0)
    return valid, valid & (mk >= first_lower)


def _rms_scale(sum_sq, width):
    return lax.rsqrt(sum_sq * (1.0 / width) + RMS_EPS)


def _scores(q, k, valid):
    s = lax.dot_general(q, k, (((1,), (1,)), ((), ())), preferred_element_type=F32)
    return jnp.where(valid, s, -jnp.inf)


def _attn_a_kernel(q4_ref, k4_ref, k4h_ref, v4_ref, v4h_ref,
                   q16_ref, k16_ref, k16h_ref, v16_ref, v16h_ref, y_ref,
                   k4buf, v4buf, k16buf, v16buf,
                   acc_n, m_n, l_n, acc_m, m_m, l_m, o_s, y_s):
    first = pl.program_id(0) == 0
    k4buf[:, 0:BLOCK, :] = k4h_ref[...]
    k4buf[:, BLOCK:, :] = k4_ref[...]
    v4buf[:, 0:BLOCK, :] = v4h_ref[...]
    v4buf[:, BLOCK:, :] = v4_ref[...]
    k16buf[:, 0:BLOCK, :] = k16h_ref[...]
    k16buf[:, BLOCK:, :] = k16_ref[...]
    v16buf[:, 0:BLOCK, :] = v16h_ref[...]
    v16buf[:, BLOCK:, :] = v16_ref[...]

    near, near_first = _near_masks(first)
    band, band_first = _band_masks(0, first)
    full = (BLOCK, LANES)
    ones = jnp.ones((2 * BLOCK, LANES), BF16)

    work = []

    def near_item(j):
        r0 = j * NEAR_SLAB
        h0 = BLOCK + r0 - NEAR_SLAB

        def gather(ref, start, size):
            return jnp.concatenate([ref[c, start:start + size, :] for c in range(MID_DIL)], axis=0)

        def store(m, l, acc):
            for c in range(MID_DIL):
                dst = slice(c * A_MID_ROWS + r0, c * A_MID_ROWS + r0 + NEAR_SLAB)
                src = slice(c * NEAR_SLAB, (c + 1) * NEAR_SLAB)
                acc_n[dst, :] = acc[src]
                m_n[dst, :] = m[src]
                l_n[dst, :] = l[src]

        return (lambda: gather(q4_ref, r0, NEAR_SLAB), lambda: gather(k4buf, h0, 2 * NEAR_SLAB),
                lambda: gather(v4buf, h0, 2 * NEAR_SLAB), near_first if j == 0 else near, None, store)

    def mid_item(c, j):
        rows = slice(c * A_MID_ROWS + j * BLOCK, c * A_MID_ROWS + (j + 1) * BLOCK)

        def store(m, l, acc):
            acc_m[rows, :] = acc
            m_m[rows, :] = m
            l_m[rows, :] = l

        return (lambda: q4_ref[c, j * BLOCK:(j + 1) * BLOCK, :],
                lambda: k4buf[c, j * BLOCK:(j + 2) * BLOCK, :],
                lambda: v4buf[c, j * BLOCK:(j + 2) * BLOCK, :],
                band_first if j == 0 else band,
                lambda: (m_n[rows, :], l_n[rows, :], acc_n[rows, :]), store)

    def far_item(c, c2):
        r = c + MID_DIL * c2
        rows = pl.ds(c * A_MID_ROWS + c2, BLOCK, stride=MID_DIL)

        def store(m, l, acc):
            o_s[rows, :] = acc * (1.0 / l)

        return (lambda: q16_ref[r], lambda: k16buf[r], lambda: v16buf[r], band_first,
                lambda: (m_m[rows, :], l_m[rows, :], acc_m[rows, :]), store)

    work += [near_item(j) for j in range(A_TILE // BLOCK)]
    work += [mid_item(c, j) for c in range(MID_DIL) for j in range(A_MID_ROWS // BLOCK)]
    work += [far_item(c, c2) for c in range(MID_DIL) for c2 in range(FAR_DIL // MID_DIL)]
    groups = [work[g:g + GROUP] for g in range(0, len(work), GROUP)]

    def score_group(items):
        return [_scores(q(), k(), valid) for q, k, _, valid, _, _ in items]

    def finish_group(items, scores):
        staged = []
        for (_, _, _, _, old, _), s in zip(items, scores):
            m = jnp.max(s, axis=1, keepdims=True)
            if old is None:
                staged.append((jnp.broadcast_to(m, full), jnp.exp(s - m).astype(BF16), None))
                continue
            m_old, l_old, acc_old = old()
            m = jnp.maximum(m_old, m)
            p = jnp.concatenate([jnp.exp(s[:, :LANES] - m), jnp.exp(s[:, LANES:] - m)], axis=1)
            staged.append((m, p.astype(BF16), (jnp.exp(m_old - m), l_old, acc_old)))
        for (_, _, v, _, _, store), (m, p, old) in zip(items, staged):
            pv = jnp.dot(p, jnp.concatenate([v(), ones], axis=1), preferred_element_type=F32)
            acc, l = pv[:, :LANES], pv[:, LANES:]
            if old is not None:
                rescale, l_old, acc_old = old
                acc = rescale * acc_old + acc
                l = rescale * l_old + l
            store(m, l, acc)

    scores = score_group(groups[0])
    for g, items in enumerate(groups):
        upcoming = score_group(groups[g + 1]) if g + 1 < len(groups) else None
        finish_group(items, scores)
        scores = upcoming

    for c in range(MID_DIL):
        y_s[pl.ds(c, A_MID_ROWS, stride=MID_DIL), :] = o_s[c * A_MID_ROWS:(c + 1) * A_MID_ROWS, :]
    y_ref[...] = y_s[...].astype(BF16)


def _attn_a_call(qkv4, qkv16):
    seq = qkv4.shape[0] * qkv4.shape[1]
    tiles = seq // A_TILE
    mid_blocks = A_MID_ROWS // BLOCK

    def mid_tile(which):
        return pl.BlockSpec((MID_DIL, A_MID_ROWS, A_HEAD_DIM),
                            lambda t, h: (0, t, which * A_HEADS + h))

    def mid_halo(which):
        return pl.BlockSpec((MID_DIL, BLOCK, A_HEAD_DIM),
                            lambda t, h: (0, jnp.maximum(t * mid_blocks - 1, 0), which * A_HEADS + h))

    def far_tile(which):
        return pl.BlockSpec((FAR_DIL, BLOCK, A_HEAD_DIM), lambda t, h: (0, t, which * A_HEADS + h))

    def far_halo(which):
        return pl.BlockSpec((FAR_DIL, BLOCK, A_HEAD_DIM),
                            lambda t, h: (0, jnp.maximum(t - 1, 0), which * A_HEADS + h))

    slab = A_TILE * A_HEAD_DIM
    vmem = (2 * (3 * slab + 2 * slab // 4) * 2
            + 2 * 5 * slab * 2
            + 2 * slab * 2
            + 2 * (slab + slab // 4) * 2 + 2 * 2 * slab * 2
            + 8 * slab * 4)
    return pl.pallas_call(
        _attn_a_kernel,
        grid=(tiles, A_HEADS),
        in_specs=[mid_tile(0), mid_tile(1), mid_halo(1), mid_tile(2), mid_halo(2),
                  far_tile(0), far_tile(1), far_halo(1), far_tile(2), far_halo(2)],
        out_specs=pl.BlockSpec((A_TILE, A_HEAD_DIM), lambda t, h: (t, h)),
        out_shape=jax.ShapeDtypeStruct((seq, A_WIDTH), BF16),
        scratch_shapes=[
            pltpu.VMEM((MID_DIL, A_MID_ROWS + BLOCK, A_HEAD_DIM), BF16),
            pltpu.VMEM((MID_DIL, A_MID_ROWS + BLOCK, A_HEAD_DIM), BF16),
            pltpu.VMEM((FAR_DIL, 2 * BLOCK, A_HEAD_DIM), BF16),
            pltpu.VMEM((FAR_DIL, 2 * BLOCK, A_HEAD_DIM), BF16),
        ] + [pltpu.VMEM((A_TILE, LANES), F32)] * 8,
        compiler_params=_params(("arbitrary", "arbitrary"), vmem),
        name="attn_a",
    )(qkv4, qkv4, qkv4, qkv4, qkv4, qkv16, qkv16, qkv16, qkv16, qkv16)


def _attn_b_kernel(sinks_ref, q_ref, km_ref, kh_ref, vm_ref, vh_ref, gain_ref, y_ref,
                   klo, khi, vlo, vhi, obuf, *, rows):
    lane = lax.broadcasted_iota(jnp.int32, (rows + BLOCK, LANES), 1)
    low = lane < B_HEAD_DIM

    def fill(dst_lo, dst_hi, halo_ref, main_ref):
        full = jnp.concatenate([halo_ref[...], main_ref[...]], axis=0).astype(F32)
        swapped = pltpu.roll(full, B_HEAD_DIM, 1)
        zero = jnp.zeros_like(full)
        for c in range(B_KV_HEADS):
            own_low, own_high = (full, swapped) if c == 0 else (swapped, full)
            dst_lo[c] = jnp.where(low, own_low, zero).astype(BF16)
            dst_hi[c] = jnp.where(low, zero, own_high).astype(BF16)

    fill(klo, khi, kh_ref, km_ref)
    fill(vlo, vhi, vh_ref, vm_ref)

    band, band_first = _band_masks(1, pl.program_id(0) == 0)

    items = [(j, pair) for j in range(rows // BLOCK) for pair in range(B_HEADS // 2)]
    groups = [items[g:g + B_ITEM_GROUP] for g in range(0, len(items), B_ITEM_GROUP)]

    def score_group(group):
        out = []
        for j, pair in group:
            c = (2 * pair) // B_GROUP
            r0 = j * BLOCK
            q = q_ref[r0:r0 + BLOCK, pair * LANES:(pair + 1) * LANES]
            valid = band_first if j == 0 else band
            out.append([_scores(q, k_sel[c, r0:r0 + 2 * BLOCK, :], valid) for k_sel in (klo, khi)])
        return out

    def finish_group(group, scores):
        staged = []
        for (j, pair), pair_scores in zip(group, scores):
            for odd, s in enumerate(pair_scores):
                m = jnp.max(s, axis=1, keepdims=True)
                p = jnp.exp(s - m)
                l = jnp.sum(p, axis=1, keepdims=True)
                sink = sinks_ref[2 * pair + odd]
                m_all = jnp.maximum(m, sink)
                e = jnp.exp(m - m_all)
                staged.append((p.astype(BF16), e / (l * e + jnp.exp(sink - m_all))))
        for i, (j, pair) in enumerate(group):
            c = (2 * pair) // B_GROUP
            r0 = j * BLOCK
            o = jnp.zeros((BLOCK, LANES), F32)
            for odd, v_sel in enumerate((vlo, vhi)):
                p, weight = staged[2 * i + odd]
                acc = jnp.dot(p, v_sel[c, r0:r0 + 2 * BLOCK, :], preferred_element_type=F32)
                o = o + acc * weight
            obuf[r0:r0 + BLOCK, pair * LANES:(pair + 1) * LANES] = o

    scores = score_group(groups[0])
    for g, group in enumerate(groups):
        upcoming = score_group(groups[g + 1]) if g + 1 < len(groups) else None
        finish_group(group, scores)
        scores = upcoming

    for j in range(rows // BLOCK):
        o = obuf[j * BLOCK:(j + 1) * BLOCK, :]
        scale = _rms_scale(jnp.sum(o * o, axis=1, keepdims=True), B_WIDTH)
        y_ref[j * BLOCK:(j + 1) * BLOCK, :] = (o * scale * gain_ref[...]).astype(BF16)


def _attn_b_call(qkv_b, sinks, gain):
    seq = qkv_b.shape[0]
    rows = B_ROWS
    blocks_per_tile = rows // BLOCK
    k_col = B_WIDTH // B_KV_WIDTH
    v_col = k_col + 1

    def kv_tile(col):
        return pl.BlockSpec((rows, B_KV_WIDTH), lambda t: (t, col))

    def kv_halo(col):
        return pl.BlockSpec((BLOCK, B_KV_WIDTH),
                            lambda t: (jnp.maximum(t * blocks_per_tile - 1, 0), col))

    kv_scratch = pltpu.VMEM((B_KV_HEADS, rows + BLOCK, LANES), BF16)
    vmem = (2 * 2 * rows * B_WIDTH * 2 + 4 * 2 * (rows + BLOCK) * LANES * 2
            + 4 * B_KV_HEADS * (rows + BLOCK) * LANES * 2 + rows * B_WIDTH * 4
            + 4 * (rows + BLOCK) * LANES * 4)
    return pl.pallas_call(
        functools.partial(_attn_b_kernel, rows=rows),
        grid=(seq // rows,),
        in_specs=[
            pl.BlockSpec(memory_space=pltpu.SMEM),
            pl.BlockSpec((rows, B_WIDTH), lambda t: (t, 0)),
            kv_tile(k_col), kv_halo(k_col), kv_tile(v_col), kv_halo(v_col),
            pl.BlockSpec((1, B_WIDTH), lambda t: (0, 0)),
        ],
        out_specs=pl.BlockSpec((rows, B_WIDTH), lambda t: (t, 0)),
        out_shape=jax.ShapeDtypeStruct((seq, B_WIDTH), BF16),
        scratch_shapes=[kv_scratch, kv_scratch, kv_scratch, kv_scratch,
                        pltpu.VMEM((rows, B_WIDTH), F32)],
        compiler_params=_params(("arbitrary",), vmem),
        name="attn_b",
    )(sinks, qkv_b, qkv_b, qkv_b, qkv_b, qkv_b, gain.reshape(1, B_WIDTH))


def _layer_norm_rows(z, g, b):
    mu = jnp.mean(z, axis=-1, keepdims=True)
    zc = z - mu
    var = jnp.mean(zc * zc, axis=-1, keepdims=True)
    return zc * lax.rsqrt(var + LN_EPS) * g + b


def _out_kernel(ya_ref, yb_ref, ga_ref, w_ref, x_ref, g_ref, b_ref, wdn_ref,
                h_ref, wdn_bf_ref):
    wdn_bf_ref[...] = wdn_ref[...].astype(BF16)
    blocks = [slice(r, r + OUT_SUB_ROWS) for r in range(0, ya_ref.shape[0], OUT_SUB_ROWS)]
    mixes = []
    for rows in blocks:
        ya = ya_ref[rows, :].astype(F32)
        scale = _rms_scale(jnp.sum(ya * ya, axis=1, keepdims=True), A_WIDTH)
        ya = (ya * scale * ga_ref[...]).astype(BF16)
        mix = jnp.dot(ya, w_ref[0:A_WIDTH, :], preferred_element_type=F32)
        mixes.append(mix + jnp.dot(yb_ref[rows, :], w_ref[A_WIDTH:, :], preferred_element_type=F32))
    for rows, mix in zip(blocks, mixes):
        h_ref[rows, :] = _layer_norm_rows(ALPHA * x_ref[rows, :] + mix, g_ref[...], b_ref[...])


def _out_call(ya, yb, gain_a, w_out_bf, x2d, ln_g, ln_b, w_down):
    seq = x2d.shape[0]
    tm = OUT_ROWS
    steps = seq // tm
    down_rows = D_FF // steps
    row_spec = pl.BlockSpec((1, D_MODEL), lambda i: (0, 0))
    vmem = (D_MODEL * D_MODEL * 2 + 2 * 2 * tm * A_WIDTH * 2
            + 2 * 2 * tm * D_MODEL * 4 + 2 * tm * D_MODEL * 4
            + 2 * down_rows * D_MODEL * 6)
    return pl.pallas_call(
        _out_kernel,
        grid=(steps,),
        in_specs=[
            pl.BlockSpec((tm, A_WIDTH), lambda i: (i, 0)),
            pl.BlockSpec((tm, B_WIDTH), lambda i: (i, 0)),
            pl.BlockSpec((1, A_WIDTH), lambda i: (0, 0)),
            pl.BlockSpec((D_MODEL, D_MODEL), lambda i: (0, 0), pipeline_mode=pl.Buffered(1)),
            pl.BlockSpec((tm, D_MODEL), lambda i: (i, 0)),
            row_spec, row_spec,
            pl.BlockSpec((down_rows, D_MODEL), lambda i: (i, 0)),
        ],
        out_specs=[
            pl.BlockSpec((tm, D_MODEL), lambda i: (i, 0)),
            pl.BlockSpec((down_rows, D_MODEL), lambda i: (i, 0)),
        ],
        out_shape=[
            jax.ShapeDtypeStruct((seq, D_MODEL), F32),
            jax.ShapeDtypeStruct((D_FF, D_MODEL), BF16),
        ],
        compiler_params=_params(("arbitrary",), vmem),
        name="out_proj_ln",
    )(ya, yb, gain_a.reshape(1, A_WIDTH), w_out_bf, x2d,
      ln_g.reshape(1, D_MODEL), ln_b.reshape(1, D_MODEL), w_down)


def _ffn_kernel(h_ref, wg_ref, wu_ref, wd_ref, g_ref, b_ref, o_ref, hb_ref):
    j = pl.program_id(1)

    @pl.when(j == 0)
    def _():
        h = h_ref[...]
        hb_ref[...] = h.astype(BF16)
        o_ref[...] = ALPHA * h

    hb = hb_ref[...]
    half = wg_ref.shape[1] // 2
    halves = [slice(0, half), slice(half, 2 * half)]
    gates = [jnp.dot(hb, wg_ref[:, s], preferred_element_type=F32) for s in halves]
    ups = [jnp.dot(hb, wu_ref[:, s], preferred_element_type=F32) for s in halves]
    acts = [(g * jax.nn.sigmoid(g) * u).astype(BF16) for g, u in zip(gates, ups)]
    down = [jnp.dot(a, wd_ref[s, :], preferred_element_type=F32) for a, s in zip(acts, halves)]
    o_ref[...] += down[0] + down[1]

    @pl.when(j == pl.num_programs(1) - 1)
    def _():
        o_ref[...] = _layer_norm_rows(o_ref[...], g_ref[...], b_ref[...])


def _ffn_call(h, w_up_b, w_down_b, ln_g, ln_b):
    seq = h.shape[0]
    tm, tf = FFN_ROWS, FFN_CHUNK
    n_chunks = D_FF // tf
    row_spec = pl.BlockSpec((1, D_MODEL), lambda i, j: (0, 0))
    vmem = (2 * tm * D_MODEL * 4 + tm * D_MODEL * 2 + 2 * tm * D_MODEL * 4
            + 2 * 3 * D_MODEL * tf * 2 + 3 * tm * tf * 4)
    return pl.pallas_call(
        _ffn_kernel,
        grid=(seq // tm, n_chunks),
        in_specs=[
            pl.BlockSpec((tm, D_MODEL), lambda i, j: (i, 0)),
            pl.BlockSpec((D_MODEL, tf), lambda i, j: (0, j)),
            pl.BlockSpec((D_MODEL, tf), lambda i, j: (0, n_chunks + j)),
            pl.BlockSpec((tf, D_MODEL), lambda i, j: (j, 0)),
            row_spec, row_spec,
        ],
        out_specs=pl.BlockSpec((tm, D_MODEL), lambda i, j: (i, 0)),
        out_shape=jax.ShapeDtypeStruct((seq, D_MODEL), F32),
        scratch_shapes=[pltpu.VMEM((tm, D_MODEL), BF16)],
        compiler_params=_params(("arbitrary", "arbitrary"), vmem),
        name="swiglu_ln",
    )(h, w_up_b, w_up_b, w_down_b, ln_g.reshape(1, D_MODEL), ln_b.reshape(1, D_MODEL))


def _layer(x2d, w_in, b_in, sinks, g_mix_a, g_mix_b, w_out, ln1_g, ln1_b, w_up, w_down, ln2_g, ln2_b):
    qkv4, qkv16, qkv_b, w_out_b, w_up_b = _proj_call(x2d, w_in, b_in, w_out, w_up)
    ya = _attn_a_call(qkv4, qkv16)
    yb = _attn_b_call(qkv_b, sinks, g_mix_b)
    h, w_down_b = _out_call(ya, yb, g_mix_a, w_out_b, x2d, ln1_g, ln1_b, w_down)
    return _ffn_call(h, w_up_b, w_down_b, ln2_g, ln2_b)


def kernel(x, w_in, b_in, sinks, g_mix_a, g_mix_b, w_out, ln1_g, ln1_b, w_up, w_down, ln2_g, ln2_b):
    batch, seq, d_model = x.shape
    assert (batch, d_model) == (1, D_MODEL) and seq % A_TILE == 0
    h = x.reshape(seq, d_model)
    for i in range(w_in.shape[0]):
        h = _layer(h, w_in[i], b_in[i], sinks[i], g_mix_a[i], g_mix_b[i], w_out[i],
                   ln1_g[i], ln1_b[i], w_up[i], w_down[i], ln2_g[i], ln2_b[i])
    return h.reshape(batch, seq, d_model)
```

```python
import functools
import math

import jax
import jax.numpy as jnp
from jax import lax
from jax.experimental import pallas as pl
from jax.experimental.pallas import tpu as pltpu

F32 = jnp.float32
BF16 = jnp.bfloat16

D_MODEL = 2048
A_HEAD_DIM = 128
A_WIDTH = 1024
A_HEADS = 8
MID_DIL = 4
FAR_DIL = 16
B_HEAD_DIM = 64
B_WIDTH = 1024
B_HEADS = 16
B_KV_HEADS = 2
B_GROUP = 8
B_KV_WIDTH = 128
QKV_A_WIDTH = 3 * A_WIDTH
QKV_B_WIDTH = B_WIDTH + 2 * B_KV_WIDTH
IN_WIDTH = QKV_A_WIDTH + QKV_B_WIDTH
D_FF = 5632
BLOCK = 128
ROPE_THETA = 10000.0
ALPHA = 2.0 ** 0.25
LN_EPS = 1e-5
RMS_EPS = 1e-6
LOG2_E = math.log2(math.e)

LANES = 128
V7X_VMEM_BYTES = 64 * 1024 * 1024
V7X_VMEM_REQUEST_CAP = V7X_VMEM_BYTES - 2 * 1024 * 1024

PROJ_ROWS = 512
PROJ_CHUNK = 512
A_TILE = FAR_DIL * BLOCK
B_ROWS = 512
OUT_ROWS = 512
OUT_SUB_ROWS = 256
FFN_ROWS = 1024
FFN_CHUNK = 512

A_MID_ROWS = A_TILE // MID_DIL
NEAR_SLAB = BLOCK // MID_DIL
GROUP = 4
B_ITEM_GROUP = 2


def _vmem_limit(estimate_bytes):
    return int(min(V7X_VMEM_REQUEST_CAP, estimate_bytes + estimate_bytes // 4))


def _params(semantics, vmem_estimate):
    return pltpu.CompilerParams(dimension_semantics=semantics,
                                vmem_limit_bytes=_vmem_limit(vmem_estimate))


def _rope_factors(seq, tile, head_dim):
    half = head_dim // 2
    inv_freq = ROPE_THETA ** (-jnp.arange(half, dtype=F32) / half)
    reps = LANES // head_dim

    def expand(pos):
        ang = pos.astype(F32)[:, None] * inv_freq[None, :]
        cos = jnp.cos(ang)
        sin = jnp.sin(ang)
        return (jnp.tile(jnp.concatenate([cos, cos], axis=-1), (1, reps)),
                jnp.tile(jnp.concatenate([sin, sin], axis=-1), (1, reps)))

    return expand(jnp.arange(tile)) + expand(jnp.arange(seq // tile) * tile)


def _proj_kernel(x_ref, w_ref, b_ref, oca_ref, osa_ref, tca_ref, tsa_ref,
                 ocb_ref, osb_ref, tcb_ref, tsb_ref, wout_ref, wup_ref,
                 o4_ref, o16_ref, ob_ref, wout_bf_ref, wup_bf_ref, nat_buf, mid_buf):
    rows = x_ref.shape[0]
    step = pl.program_id(0)
    wout_bf_ref[...] = wout_ref[...].astype(BF16)
    wup_bf_ref[...] = wup_ref[...].astype(BF16)
    xb = x_ref[...].astype(BF16)
    lane = lax.broadcasted_iota(jnp.int32, (rows, LANES), 1)
    first_half_a = lane < (A_HEAD_DIM // 2)
    first_half_b = (lane % B_HEAD_DIM) < (B_HEAD_DIM // 2)

    def tables(off_c_ref, off_s_ref, start_c_ref, start_s_ref, first_half):
        oc, os_ = off_c_ref[...], off_s_ref[...]
        tc, ts = start_c_ref[pl.ds(step, 1), :], start_s_ref[pl.ds(step, 1), :]
        sin = ts * oc + tc * os_
        return tc * oc - ts * os_, jnp.where(first_half, -sin, sin)

    ca, sa = tables(oca_ref, osa_ref, tca_ref, tsa_ref, first_half_a)
    cb, sb = tables(ocb_ref, osb_ref, tcb_ref, tsb_ref, first_half_b)
    scale_a = LOG2_E / math.sqrt(A_HEAD_DIM)
    scale_b = LOG2_E / math.sqrt(B_HEAD_DIM)
    mid_rows = rows // MID_DIL
    far_rows = rows // FAR_DIL

    def rope_a(y):
        return y * ca + pltpu.roll(y, A_HEAD_DIM // 2, 1) * sa

    def rope_b(y):
        partner = jnp.where(first_half_b,
                            pltpu.roll(y, LANES - B_HEAD_DIM // 2, 1),
                            pltpu.roll(y, B_HEAD_DIM // 2, 1))
        return y * cb + partner * sb

    for c0 in range(0, IN_WIDTH, PROJ_CHUNK):
        c1 = min(c0 + PROJ_CHUNK, IN_WIDTH)
        acc = jnp.dot(xb, w_ref[:, c0:c1], preferred_element_type=F32) + b_ref[:, c0:c1]
        for g, col in enumerate(range(c0, c1, LANES)):
            y = acc[:, col - c0:col - c0 + LANES]
            if col < A_WIDTH:
                y = rope_a(y) * scale_a
            elif col < 2 * A_WIDTH:
                y = rope_a(y)
            elif col < QKV_A_WIDTH:
                pass
            elif col < QKV_A_WIDTH + B_WIDTH:
                y = rope_b(y) * scale_b
            elif col < QKV_A_WIDTH + B_WIDTH + B_KV_WIDTH:
                y = rope_b(y)
            if col >= QKV_A_WIDTH:
                ob_ref[:, col - QKV_A_WIDTH:col - QKV_A_WIDTH + LANES] = y.astype(BF16)
                continue
            nat_buf[g] = y
            for c in range(MID_DIL):
                cls = nat_buf[g, pl.ds(c, mid_rows, stride=MID_DIL), :]
                o4_ref[c, :, col:col + LANES] = cls.astype(BF16)
                mid_buf[g, c * mid_rows:(c + 1) * mid_rows, :] = cls
            for c in range(MID_DIL):
                for c2 in range(FAR_DIL // MID_DIL):
                    cls = mid_buf[g, pl.ds(c * mid_rows + c2, far_rows, stride=MID_DIL), :]
                    o16_ref[c + MID_DIL * c2, :, col:col + LANES] = cls.astype(BF16)


def _proj_call(x2d, w_in, b_in, w_out, w_up):
    seq = x2d.shape[0]
    tm = PROJ_ROWS
    steps = seq // tm
    rope_a = _rope_factors(seq, tm, A_HEAD_DIM)
    rope_b = _rope_factors(seq, tm, B_HEAD_DIM)
    offset_spec = pl.BlockSpec((tm, LANES), lambda i: (0, 0))
    start_spec = pl.BlockSpec((steps, LANES), lambda i: (0, 0))
    rope_specs = [offset_spec, offset_spec, start_spec, start_spec]
    wout_rows = D_MODEL // steps
    groups = PROJ_CHUNK // LANES
    vmem = (D_MODEL * IN_WIDTH * 2
            + 2 * tm * D_MODEL * 4
            + 2 * tm * (2 * QKV_A_WIDTH + QKV_B_WIDTH) * 2
            + 2 * 4 * (tm + steps) * LANES * 4
            + 2 * wout_rows * (D_MODEL + 2 * D_FF) * 6
            + 2 * groups * tm * LANES * 4
            + tm * D_MODEL * 2 + 2 * tm * PROJ_CHUNK * 4)
    return pl.pallas_call(
        _proj_kernel,
        grid=(steps,),
        in_specs=[
            pl.BlockSpec((tm, D_MODEL), lambda i: (i, 0)),
            pl.BlockSpec((D_MODEL, IN_WIDTH), lambda i: (0, 0), pipeline_mode=pl.Buffered(1)),
            pl.BlockSpec((1, IN_WIDTH), lambda i: (0, 0)),
            *rope_specs, *rope_specs,
            pl.BlockSpec((wout_rows, D_MODEL), lambda i: (i, 0)),
            pl.BlockSpec((wout_rows, 2 * D_FF), lambda i: (i, 0)),
        ],
        out_specs=[
            pl.BlockSpec((MID_DIL, tm // MID_DIL, QKV_A_WIDTH), lambda i: (0, i, 0)),
            pl.BlockSpec((FAR_DIL, tm // FAR_DIL, QKV_A_WIDTH), lambda i: (0, i, 0)),
            pl.BlockSpec((tm, QKV_B_WIDTH), lambda i: (i, 0)),
            pl.BlockSpec((wout_rows, D_MODEL), lambda i: (i, 0)),
            pl.BlockSpec((wout_rows, 2 * D_FF), lambda i: (i, 0)),
        ],
        out_shape=[
            jax.ShapeDtypeStruct((MID_DIL, seq // MID_DIL, QKV_A_WIDTH), BF16),
            jax.ShapeDtypeStruct((FAR_DIL, seq // FAR_DIL, QKV_A_WIDTH), BF16),
            jax.ShapeDtypeStruct((seq, QKV_B_WIDTH), BF16),
            jax.ShapeDtypeStruct((D_MODEL, D_MODEL), BF16),
            jax.ShapeDtypeStruct((D_MODEL, 2 * D_FF), BF16),
        ],
        scratch_shapes=[pltpu.VMEM((groups, tm, LANES), F32), pltpu.VMEM((groups, tm, LANES), F32)],
        compiler_params=_params(("arbitrary",), vmem),
        name="proj_rope",
    )(x2d, w_in.astype(BF16), b_in.reshape(1, IN_WIDTH), *rope_a, *rope_b, w_out, w_up)


def _band_masks(lo, is_first_tile):
    qi = lax.broadcasted_iota(jnp.int32, (BLOCK, 2 * BLOCK), 0)
    kj = lax.broadcasted_iota(jnp.int32, (BLOCK, 2 * BLOCK), 1)
    band = (kj >= qi + lo) & (kj <= qi + BLOCK)
    first_lower = jnp.where(is_first_tile, BLOCK, 0)
    return band, band & (kj >= first_lower)


def _near_masks(is_first_tile):
    iq = lax.broadcasted_iota(jnp.int32, (BLOCK, 2 * BLOCK), 0)
    ik = lax.broadcasted_iota(jnp.int32, (BLOCK, 2 * BLOCK), 1)
    cq, mq = iq // NEAR_SLAB, iq % NEAR_SLAB
    ck, mk = ik // (2 * NEAR_SLAB), ik % (2 * NEAR_SLAB)
    dist = MID_DIL * (mq + NEAR_SLAB - mk) + (cq - ck)
    valid = (dist >= 0) & (dist <= BLOCK)
    first_lower = jnp.where(is_first_tile, NEAR_SLAB, 0)
    return valid, valid & (mk >= first_lower)


def _rms_scale(sum_sq, width):
    return lax.rsqrt(sum_sq * (1.0 / width) + RMS_EPS)


def _scores(q, k, valid):
    s = lax.dot_general(q, k, (((1,), (1,)), ((), ())), preferred_element_type=F32)
    return jnp.where(valid, s, -jnp.inf)


def _attn_a_kernel(q4_ref, k4_ref, k4h_ref, v4_ref, v4h_ref,
                   q16_ref, k16_ref, k16h_ref, v16_ref, v16h_ref, y_ref,
                   k4buf, v4buf, k16buf, v16buf,
                   acc_n, m_n, l_n, acc_m, m_m, l_m, o_s, y_s):
    first = pl.program_id(0) == 0
    k4buf[:, 0:BLOCK, :] = k4h_ref[...]
    k4buf[:, BLOCK:, :] = k4_ref[...]
    v4buf[:, 0:BLOCK, :] = v4h_ref[...]
    v4buf[:, BLOCK:, :] = v4_ref[...]
    k16buf[:, 0:BLOCK, :] = k16h_ref[...]
    k16buf[:, BLOCK:, :] = k16_ref[...]
    v16buf[:, 0:BLOCK, :] = v16h_ref[...]
    v16buf[:, BLOCK:, :] = v16_ref[...]

    near, near_first = _near_masks(first)
    band, band_first = _band_masks(0, first)
    full = (BLOCK, LANES)
    ones = jnp.ones((2 * BLOCK, LANES), BF16)

    work = []

    def near_item(j):
        r0 = j * NEAR_SLAB
        h0 = BLOCK + r0 - NEAR_SLAB

        def gather(ref, start, size):
            return jnp.concatenate([ref[c, start:start + size, :] for c in range(MID_DIL)], axis=0)

        def store(m, l, acc):
            for c in range(MID_DIL):
                dst = slice(c * A_MID_ROWS + r0, c * A_MID_ROWS + r0 + NEAR_SLAB)
                src = slice(c * NEAR_SLAB, (c + 1) * NEAR_SLAB)
                acc_n[dst, :] = acc[src]
                m_n[dst, :] = m[src]
                l_n[dst, :] = l[src]

        return (lambda: gather(q4_ref, r0, NEAR_SLAB), lambda: gather(k4buf, h0, 2 * NEAR_SLAB),
                lambda: gather(v4buf, h0, 2 * NEAR_SLAB), near_first if j == 0 else near, None, store)

    def mid_item(c, j):
        rows = slice(c * A_MID_ROWS + j * BLOCK, c * A_MID_ROWS + (j + 1) * BLOCK)

        def store(m, l, acc):
            acc_m[rows, :] = acc
            m_m[rows, :] = m
            l_m[rows, :] = l

        return (lambda: q4_ref[c, j * BLOCK:(j + 1) * BLOCK, :],
                lambda: k4buf[c, j * BLOCK:(j + 2) * BLOCK, :],
                lambda: v4buf[c, j * BLOCK:(j + 2) * BLOCK, :],
                band_first if j == 0 else band,
                lambda: (m_n[rows, :], l_n[rows, :], acc_n[rows, :]), store)

    def far_item(c, c2):
        r = c + MID_DIL * c2
        rows = pl.ds(c * A_MID_ROWS + c2, BLOCK, stride=MID_DIL)

        def store(m, l, acc):
            o_s[rows, :] = acc * (1.0 / l)

        return (lambda: q16_ref[r], lambda: k16buf[r], lambda: v16buf[r], band_first,
                lambda: (m_m[rows, :], l_m[rows, :], acc_m[rows, :]), store)

    work += [near_item(j) for j in range(A_TILE // BLOCK)]
    work += [mid_item(c, j) for c in range(MID_DIL) for j in range(A_MID_ROWS // BLOCK)]
    work += [far_item(c, c2) for c in range(MID_DIL) for c2 in range(FAR_DIL // MID_DIL)]
    groups = [work[g:g + GROUP] for g in range(0, len(work), GROUP)]

    def score_group(items):
        return [_scores(q(), k(), valid) for q, k, _, valid, _, _ in items]

    def finish_group(items, scores):
        staged = []
        for (_, _, _, _, old, _), s in zip(items, scores):
            m = jnp.max(s, axis=1, keepdims=True)
            if old is None:
                staged.append((jnp.broadcast_to(m, full), jnp.exp2(s - m).astype(BF16), None))
                continue
            m_old, l_old, acc_old = old()
            m = jnp.maximum(m_old, m)
            p = jnp.concatenate([jnp.exp2(s[:, :LANES] - m), jnp.exp2(s[:, LANES:] - m)], axis=1)
            staged.append((m, p.astype(BF16), (jnp.exp2(m_old - m), l_old, acc_old)))
        for (_, _, v, _, _, store), (m, p, old) in zip(items, staged):
            pv = jnp.dot(p, jnp.concatenate([v(), ones], axis=1), preferred_element_type=F32)
            acc, l = pv[:, :LANES], pv[:, LANES:]
            if old is not None:
                rescale, l_old, acc_old = old
                acc = rescale * acc_old + acc
                l = rescale * l_old + l
            store(m, l, acc)

    scores = score_group(groups[0])
    for g, items in enumerate(groups):
        upcoming = score_group(groups[g + 1]) if g + 1 < len(groups) else None
        finish_group(items, scores)
        scores = upcoming

    for c in range(MID_DIL):
        y_s[pl.ds(c, A_MID_ROWS, stride=MID_DIL), :] = o_s[c * A_MID_ROWS:(c + 1) * A_MID_ROWS, :]
    y_ref[...] = y_s[...].astype(BF16)


def _attn_a_call(qkv4, qkv16):
    seq = qkv4.shape[0] * qkv4.shape[1]
    tiles = seq // A_TILE
    mid_blocks = A_MID_ROWS // BLOCK

    def mid_tile(which):
        return pl.BlockSpec((MID_DIL, A_MID_ROWS, A_HEAD_DIM),
                            lambda t, h: (0, t, which * A_HEADS + h))

    def mid_halo(which):
        return pl.BlockSpec((MID_DIL, BLOCK, A_HEAD_DIM),
                            lambda t, h: (0, jnp.maximum(t * mid_blocks - 1, 0), which * A_HEADS + h))

    def far_tile(which):
        return pl.BlockSpec((FAR_DIL, BLOCK, A_HEAD_DIM), lambda t, h: (0, t, which * A_HEADS + h))

    def far_halo(which):
        return pl.BlockSpec((FAR_DIL, BLOCK, A_HEAD_DIM),
                            lambda t, h: (0, jnp.maximum(t - 1, 0), which * A_HEADS + h))

    slab = A_TILE * A_HEAD_DIM
    vmem = (2 * (3 * slab + 2 * slab // 4) * 2
            + 2 * 5 * slab * 2
            + 2 * slab * 2
            + 2 * (slab + slab // 4) * 2 + 2 * 2 * slab * 2
            + 8 * slab * 4)
    return pl.pallas_call(
        _attn_a_kernel,
        grid=(tiles, A_HEADS),
        in_specs=[mid_tile(0), mid_tile(1), mid_halo(1), mid_tile(2), mid_halo(2),
                  far_tile(0), far_tile(1), far_halo(1), far_tile(2), far_halo(2)],
        out_specs=pl.BlockSpec((A_TILE, A_HEAD_DIM), lambda t, h: (t, h)),
        out_shape=jax.ShapeDtypeStruct((seq, A_WIDTH), BF16),
        scratch_shapes=[
            pltpu.VMEM((MID_DIL, A_MID_ROWS + BLOCK, A_HEAD_DIM), BF16),
            pltpu.VMEM((MID_DIL, A_MID_ROWS + BLOCK, A_HEAD_DIM), BF16),
            pltpu.VMEM((FAR_DIL, 2 * BLOCK, A_HEAD_DIM), BF16),
            pltpu.VMEM((FAR_DIL, 2 * BLOCK, A_HEAD_DIM), BF16),
        ] + [pltpu.VMEM((A_TILE, LANES), F32)] * 8,
        compiler_params=_params(("arbitrary", "arbitrary"), vmem),
        name="attn_a",
    )(qkv4, qkv4, qkv4, qkv4, qkv4, qkv16, qkv16, qkv16, qkv16, qkv16)


def _attn_b_kernel(sinks_ref, q_ref, km_ref, kh_ref, vm_ref, vh_ref, gain_ref, y_ref,
                   klo, khi, vlo, vhi, obuf, *, rows):
    lane = lax.broadcasted_iota(jnp.int32, (rows + BLOCK, LANES), 1)
    low = lane < B_HEAD_DIM

    def fill(dst_lo, dst_hi, halo_ref, main_ref):
        full = jnp.concatenate([halo_ref[...], main_ref[...]], axis=0).astype(F32)
        swapped = pltpu.roll(full, B_HEAD_DIM, 1)
        zero = jnp.zeros_like(full)
        for c in range(B_KV_HEADS):
            own_low, own_high = (full, swapped) if c == 0 else (swapped, full)
            dst_lo[c] = jnp.where(low, own_low, zero).astype(BF16)
            dst_hi[c] = jnp.where(low, zero, own_high).astype(BF16)

    fill(klo, khi, kh_ref, km_ref)
    fill(vlo, vhi, vh_ref, vm_ref)

    band, band_first = _band_masks(1, pl.program_id(0) == 0)

    items = [(j, pair) for j in range(rows // BLOCK) for pair in range(B_HEADS // 2)]
    groups = [items[g:g + B_ITEM_GROUP] for g in range(0, len(items), B_ITEM_GROUP)]

    col0 = lax.broadcasted_iota(jnp.int32, (1, LANES), 1) == 0
    neg_inf_row = jnp.full((1, LANES), -jnp.inf, F32)

    def score_group(group):
        out = []
        for j, pair in group:
            c = (2 * pair) // B_GROUP
            r0 = j * BLOCK
            q = q_ref[r0:r0 + BLOCK, pair * LANES:(pair + 1) * LANES]
            valid = band_first if j == 0 else band
            pair_scores = []
            for odd, k_sel in enumerate((klo, khi)):
                s = lax.dot_general(q, k_sel[c, r0:r0 + 2 * BLOCK, :], (((1,), (1,)), ((), ())),
                                    preferred_element_type=F32)
                sink_row = jnp.where(col0, sinks_ref[2 * pair + odd] * LOG2_E, neg_inf_row)
                pair_scores.append(jnp.concatenate(
                    [jnp.where(valid[:, :LANES], s[:, :LANES], sink_row),
                     jnp.where(valid[:, LANES:], s[:, LANES:], -jnp.inf)], axis=1))
            out.append(pair_scores)
        return out

    def finish_group(group, scores):
        staged = []
        for (j, pair), pair_scores in zip(group, scores):
            for odd, s in enumerate(pair_scores):
                m = jnp.max(s, axis=1, keepdims=True)
                p = jnp.exp2(s - m)
                l = jnp.sum(p, axis=1, keepdims=True)
                p = jnp.concatenate([jnp.where(col0, 0.0, p[:, :LANES]), p[:, LANES:]], axis=1)
                staged.append((p.astype(BF16), 1.0 / l))
        for i, (j, pair) in enumerate(group):
            c = (2 * pair) // B_GROUP
            r0 = j * BLOCK
            o = jnp.zeros((BLOCK, LANES), F32)
            for odd, v_sel in enumerate((vlo, vhi)):
                p, weight = staged[2 * i + odd]
                acc = jnp.dot(p, v_sel[c, r0:r0 + 2 * BLOCK, :], preferred_element_type=F32)
                o = o + acc * weight
            obuf[r0:r0 + BLOCK, pair * LANES:(pair + 1) * LANES] = o

    scores = score_group(groups[0])
    for g, group in enumerate(groups):
        upcoming = score_group(groups[g + 1]) if g + 1 < len(groups) else None
        finish_group(group, scores)
        scores = upcoming

    for j in range(rows // BLOCK):
        o = obuf[j * BLOCK:(j + 1) * BLOCK, :]
        scale = _rms_scale(jnp.sum(o * o, axis=1, keepdims=True), B_WIDTH)
        y_ref[j * BLOCK:(j + 1) * BLOCK, :] = (o * scale * gain_ref[...]).astype(BF16)


def _attn_b_call(qkv_b, sinks, gain):
    seq = qkv_b.shape[0]
    rows = B_ROWS
    blocks_per_tile = rows // BLOCK
    k_col = B_WIDTH // B_KV_WIDTH
    v_col = k_col + 1

    def kv_tile(col):
        return pl.BlockSpec((rows, B_KV_WIDTH), lambda t: (t, col))

    def kv_halo(col):
        return pl.BlockSpec((BLOCK, B_KV_WIDTH),
                            lambda t: (jnp.maximum(t * blocks_per_tile - 1, 0), col))

    kv_scratch = pltpu.VMEM((B_KV_HEADS, rows + BLOCK, LANES), BF16)
    vmem = (2 * 2 * rows * B_WIDTH * 2 + 4 * 2 * (rows + BLOCK) * LANES * 2
            + 4 * B_KV_HEADS * (rows + BLOCK) * LANES * 2 + rows * B_WIDTH * 4
            + 4 * (rows + BLOCK) * LANES * 4)
    return pl.pallas_call(
        functools.partial(_attn_b_kernel, rows=rows),
        grid=(seq // rows,),
        in_specs=[
            pl.BlockSpec(memory_space=pltpu.SMEM),
            pl.BlockSpec((rows, B_WIDTH), lambda t: (t, 0)),
            kv_tile(k_col), kv_halo(k_col), kv_tile(v_col), kv_halo(v_col),
            pl.BlockSpec((1, B_WIDTH), lambda t: (0, 0)),
        ],
        out_specs=pl.BlockSpec((rows, B_WIDTH), lambda t: (t, 0)),
        out_shape=jax.ShapeDtypeStruct((seq, B_WIDTH), BF16),
        scratch_shapes=[kv_scratch, kv_scratch, kv_scratch, kv_scratch,
                        pltpu.VMEM((rows, B_WIDTH), F32)],
        compiler_params=_params(("arbitrary",), vmem),
        name="attn_b",
    )(sinks, qkv_b, qkv_b, qkv_b, qkv_b, qkv_b, gain.reshape(1, B_WIDTH))


def _layer_norm_rows(z, g, b):
    mu = jnp.mean(z, axis=-1, keepdims=True)
    zc = z - mu
    var = jnp.mean(zc * zc, axis=-1, keepdims=True)
    return zc * lax.rsqrt(var + LN_EPS) * g + b


def _out_kernel(ya_ref, yb_ref, ga_ref, w_ref, x_ref, g_ref, b_ref, wdn_ref,
                h_ref, wdn_bf_ref):
    wdn_bf_ref[...] = wdn_ref[...].astype(BF16)
    blocks = [slice(r, r + OUT_SUB_ROWS) for r in range(0, ya_ref.shape[0], OUT_SUB_ROWS)]
    mixes = []
    for rows in blocks:
        ya = ya_ref[rows, :].astype(F32)
        scale = _rms_scale(jnp.sum(ya * ya, axis=1, keepdims=True), A_WIDTH)
        ya = (ya * scale * ga_ref[...]).astype(BF16)
        mix = jnp.dot(ya, w_ref[0:A_WIDTH, :], preferred_element_type=F32)
        mixes.append(mix + jnp.dot(yb_ref[rows, :], w_ref[A_WIDTH:, :], preferred_element_type=F32))
    for rows, mix in zip(blocks, mixes):
        h_ref[rows, :] = _layer_norm_rows(ALPHA * x_ref[rows, :] + mix, g_ref[...], b_ref[...])


def _out_call(ya, yb, gain_a, w_out_bf, x2d, ln_g, ln_b, w_down):
    seq = x2d.shape[0]
    tm = OUT_ROWS
    steps = seq // tm
    down_rows = D_FF // steps
    row_spec = pl.BlockSpec((1, D_MODEL), lambda i: (0, 0))
    vmem = (D_MODEL * D_MODEL * 2 + 2 * 2 * tm * A_WIDTH * 2
            + 2 * 2 * tm * D_MODEL * 4 + 2 * tm * D_MODEL * 4
            + 2 * down_rows * D_MODEL * 6)
    return pl.pallas_call(
        _out_kernel,
        grid=(steps,),
        in_specs=[
            pl.BlockSpec((tm, A_WIDTH), lambda i: (i, 0)),
            pl.BlockSpec((tm, B_WIDTH), lambda i: (i, 0)),
            pl.BlockSpec((1, A_WIDTH), lambda i: (0, 0)),
            pl.BlockSpec((D_MODEL, D_MODEL), lambda i: (0, 0), pipeline_mode=pl.Buffered(1)),
            pl.BlockSpec((tm, D_MODEL), lambda i: (i, 0)),
            row_spec, row_spec,
            pl.BlockSpec((down_rows, D_MODEL), lambda i: (i, 0)),
        ],
        out_specs=[
            pl.BlockSpec((tm, D_MODEL), lambda i: (i, 0)),
            pl.BlockSpec((down_rows, D_MODEL), lambda i: (i, 0)),
        ],
        out_shape=[
            jax.ShapeDtypeStruct((seq, D_MODEL), F32),
            jax.ShapeDtypeStruct((D_FF, D_MODEL), BF16),
        ],
        compiler_params=_params(("arbitrary",), vmem),
        name="out_proj_ln",
    )(ya, yb, gain_a.reshape(1, A_WIDTH), w_out_bf, x2d,
      ln_g.reshape(1, D_MODEL), ln_b.reshape(1, D_MODEL), w_down)


def _ffn_kernel(h_ref, wg_ref, wu_ref, wd_ref, g_ref, b_ref, o_ref, hb_ref):
    j = pl.program_id(1)

    @pl.when(j == 0)
    def _():
        h = h_ref[...]
        hb_ref[...] = h.astype(BF16)
        o_ref[...] = ALPHA * h

    hb = hb_ref[...]
    half = wg_ref.shape[1] // 2
    halves = [slice(0, half), slice(half, 2 * half)]
    gates = [jnp.dot(hb, wg_ref[:, s], preferred_element_type=F32) for s in halves]
    ups = [jnp.dot(hb, wu_ref[:, s], preferred_element_type=F32) for s in halves]
    acts = [(g * jax.nn.sigmoid(g) * u).astype(BF16) for g, u in zip(gates, ups)]
    down = [jnp.dot(a, wd_ref[s, :], preferred_element_type=F32) for a, s in zip(acts, halves)]
    o_ref[...] += down[0] + down[1]

    @pl.when(j == pl.num_programs(1) - 1)
    def _():
        o_ref[...] = _layer_norm_rows(o_ref[...], g_ref[...], b_ref[...])


def _ffn_call(h, w_up_b, w_down_b, ln_g, ln_b):
    seq = h.shape[0]
    tm, tf = FFN_ROWS, FFN_CHUNK
    n_chunks = D_FF // tf
    row_spec = pl.BlockSpec((1, D_MODEL), lambda i, j: (0, 0))
    vmem = (2 * tm * D_MODEL * 4 + tm * D_MODEL * 2 + 2 * tm * D_MODEL * 4
            + 2 * 3 * D_MODEL * tf * 2 + 3 * tm * tf * 4)
    return pl.pallas_call(
        _ffn_kernel,
        grid=(seq // tm, n_chunks),
        in_specs=[
            pl.BlockSpec((tm, D_MODEL), lambda i, j: (i, 0)),
            pl.BlockSpec((D_MODEL, tf), lambda i, j: (0, j)),
            pl.BlockSpec((D_MODEL, tf), lambda i, j: (0, n_chunks + j)),
            pl.BlockSpec((tf, D_MODEL), lambda i, j: (j, 0)),
            row_spec, row_spec,
        ],
        out_specs=pl.BlockSpec((tm, D_MODEL), lambda i, j: (i, 0)),
        out_shape=jax.ShapeDtypeStruct((seq, D_MODEL), F32),
        scratch_shapes=[pltpu.VMEM((tm, D_MODEL), BF16)],
        compiler_params=_params(("arbitrary", "arbitrary"), vmem),
        name="swiglu_ln",
    )(h, w_up_b, w_up_b, w_down_b, ln_g.reshape(1, D_MODEL), ln_b.reshape(1, D_MODEL))


def _layer(x2d, w_in, b_in, sinks, g_mix_a, g_mix_b, w_out, ln1_g, ln1_b, w_up, w_down, ln2_g, ln2_b):
    qkv4, qkv16, qkv_b, w_out_b, w_up_b = _proj_call(x2d, w_in, b_in, w_out, w_up)
    ya = _attn_a_call(qkv4, qkv16)
    yb = _attn_b_call(qkv_b, sinks, g_mix_b)
    h, w_down_b = _out_call(ya, yb, g_mix_a, w_out_b, x2d, ln1_g, ln1_b, w_down)
    return _ffn_call(h, w_up_b, w_down_b, ln2_g, ln2_b)


def kernel(x, w_in, b_in, sinks, g_mix_a, g_mix_b, w_out, ln1_g, ln1_b, w_up, w_down, ln2_g, ln2_b):
    batch, seq, d_model = x.shape
    assert (batch, d_model) == (1, D_MODEL) and seq % A_TILE == 0
    h = x.reshape(seq, d_model)
    for i in range(w_in.shape[0]):
        h = _layer(h, w_in[i], b_in[i], sinks[i], g_mix_a[i], g_mix_b[i], w_out[i],
                   ln1_g[i], ln1_b[i], w_up[i], w_down[i], ln2_g[i], ln2_b[i])
    return h.reshape(batch, seq, d_model)
```

```python
import functools
import math

import jax
import jax.numpy as jnp
from jax import lax
from jax.experimental import pallas as pl
from jax.experimental.pallas import tpu as pltpu

F32 = jnp.float32
BF16 = jnp.bfloat16

D_MODEL = 2048
A_HEAD_DIM = 128
A_WIDTH = 1024
A_HEADS = 8
MID_DIL = 4
FAR_DIL = 16
B_HEAD_DIM = 64
B_WIDTH = 1024
B_HEADS = 16
B_KV_HEADS = 2
B_GROUP = 8
B_KV_WIDTH = 128
QKV_A_WIDTH = 3 * A_WIDTH
QKV_B_WIDTH = B_WIDTH + 2 * B_KV_WIDTH
IN_WIDTH = QKV_A_WIDTH + QKV_B_WIDTH
D_FF = 5632
BLOCK = 128
ROPE_THETA = 10000.0
ALPHA = 2.0 ** 0.25
LN_EPS = 1e-5
RMS_EPS = 1e-6
LOG2_E = math.log2(math.e)

LANES = 128
V7X_VMEM_BYTES = 64 * 1024 * 1024
V7X_VMEM_REQUEST_CAP = V7X_VMEM_BYTES - 2 * 1024 * 1024

PROJ_ROWS = 512
PROJ_CHUNK = 512
A_TILE = FAR_DIL * BLOCK
B_ROWS = 512
OUT_ROWS = 512
OUT_SUB_ROWS = 256
FFN_ROWS = 1024
FFN_CHUNK = 512
FFN_NORM_ROWS = 256

A_MID_ROWS = A_TILE // MID_DIL
NEAR_SLAB = BLOCK // MID_DIL
GROUP = 4
B_ITEM_GROUP = 2


def _vmem_limit(estimate_bytes):
    return int(min(V7X_VMEM_REQUEST_CAP, estimate_bytes + estimate_bytes // 4))


def _params(semantics, vmem_estimate):
    return pltpu.CompilerParams(dimension_semantics=semantics,
                                vmem_limit_bytes=_vmem_limit(vmem_estimate))


def _rope_factors(seq, tile, head_dim):
    half = head_dim // 2
    inv_freq = ROPE_THETA ** (-jnp.arange(half, dtype=F32) / half)
    reps = LANES // head_dim

    def expand(pos):
        ang = pos.astype(F32)[:, None] * inv_freq[None, :]
        cos = jnp.cos(ang)
        sin = jnp.sin(ang)
        return (jnp.tile(jnp.concatenate([cos, cos], axis=-1), (1, reps)),
                jnp.tile(jnp.concatenate([sin, sin], axis=-1), (1, reps)))

    return expand(jnp.arange(tile)) + expand(jnp.arange(seq // tile) * tile)


def _proj_kernel(x_ref, w_ref, b_ref, oca_ref, osa_ref, tca_ref, tsa_ref,
                 ocb_ref, osb_ref, tcb_ref, tsb_ref, wout_ref, wup_ref,
                 o4_ref, o16_ref, ob_ref, wout_bf_ref, wup_bf_ref, nat_buf, mid_buf):
    rows = x_ref.shape[0]
    step = pl.program_id(0)
    wout_bf_ref[...] = wout_ref[...].astype(BF16)
    wup_bf_ref[...] = wup_ref[...].astype(BF16)
    xb = x_ref[...].astype(BF16)
    lane = lax.broadcasted_iota(jnp.int32, (rows, LANES), 1)
    first_half_a = lane < (A_HEAD_DIM // 2)
    first_half_b = (lane % B_HEAD_DIM) < (B_HEAD_DIM // 2)

    def tables(off_c_ref, off_s_ref, start_c_ref, start_s_ref, first_half):
        oc, os_ = off_c_ref[...], off_s_ref[...]
        tc, ts = start_c_ref[pl.ds(step, 1), :], start_s_ref[pl.ds(step, 1), :]
        sin = ts * oc + tc * os_
        return tc * oc - ts * os_, jnp.where(first_half, -sin, sin)

    ca, sa = tables(oca_ref, osa_ref, tca_ref, tsa_ref, first_half_a)
    cb, sb = tables(ocb_ref, osb_ref, tcb_ref, tsb_ref, first_half_b)
    scale_a = LOG2_E / math.sqrt(A_HEAD_DIM)
    scale_b = LOG2_E / math.sqrt(B_HEAD_DIM)
    mid_rows = rows // MID_DIL
    far_rows = rows // FAR_DIL

    def rope_a(y):
        return y * ca + pltpu.roll(y, A_HEAD_DIM // 2, 1) * sa

    def rope_b(y):
        partner = jnp.where(first_half_b,
                            pltpu.roll(y, LANES - B_HEAD_DIM // 2, 1),
                            pltpu.roll(y, B_HEAD_DIM // 2, 1))
        return y * cb + partner * sb

    for c0 in range(0, IN_WIDTH, PROJ_CHUNK):
        c1 = min(c0 + PROJ_CHUNK, IN_WIDTH)
        acc = jnp.dot(xb, w_ref[:, c0:c1], preferred_element_type=F32) + b_ref[:, c0:c1]
        for g, col in enumerate(range(c0, c1, LANES)):
            y = acc[:, col - c0:col - c0 + LANES]
            if col < A_WIDTH:
                y = rope_a(y) * scale_a
            elif col < 2 * A_WIDTH:
                y = rope_a(y)
            elif col < QKV_A_WIDTH:
                pass
            elif col < QKV_A_WIDTH + B_WIDTH:
                y = rope_b(y) * scale_b
            elif col < QKV_A_WIDTH + B_WIDTH + B_KV_WIDTH:
                y = rope_b(y)
            if col >= QKV_A_WIDTH:
                ob_ref[:, col - QKV_A_WIDTH:col - QKV_A_WIDTH + LANES] = y.astype(BF16)
                continue
            nat_buf[g] = y
            for c in range(MID_DIL):
                cls = nat_buf[g, pl.ds(c, mid_rows, stride=MID_DIL), :]
                o4_ref[c, :, col:col + LANES] = cls.astype(BF16)
                mid_buf[g, c * mid_rows:(c + 1) * mid_rows, :] = cls
            for c in range(MID_DIL):
                for c2 in range(FAR_DIL // MID_DIL):
                    cls = mid_buf[g, pl.ds(c * mid_rows + c2, far_rows, stride=MID_DIL), :]
                    o16_ref[c + MID_DIL * c2, :, col:col + LANES] = cls.astype(BF16)


def _proj_call(x2d, w_in, b_in, w_out, w_up):
    seq = x2d.shape[0]
    tm = PROJ_ROWS
    steps = seq // tm
    rope_a = _rope_factors(seq, tm, A_HEAD_DIM)
    rope_b = _rope_factors(seq, tm, B_HEAD_DIM)
    offset_spec = pl.BlockSpec((tm, LANES), lambda i: (0, 0))
    start_spec = pl.BlockSpec((steps, LANES), lambda i: (0, 0))
    rope_specs = [offset_spec, offset_spec, start_spec, start_spec]
    wout_rows = D_MODEL // steps
    groups = PROJ_CHUNK // LANES
    vmem = (D_MODEL * IN_WIDTH * 2
            + 2 * tm * D_MODEL * 4
            + 2 * tm * (2 * QKV_A_WIDTH + QKV_B_WIDTH) * 2
            + 2 * 4 * (tm + steps) * LANES * 4
            + 2 * wout_rows * (D_MODEL + 2 * D_FF) * 6
            + 2 * groups * tm * LANES * 4
            + tm * D_MODEL * 2 + 2 * tm * PROJ_CHUNK * 4)
    return pl.pallas_call(
        _proj_kernel,
        grid=(steps,),
        in_specs=[
            pl.BlockSpec((tm, D_MODEL), lambda i: (i, 0)),
            pl.BlockSpec((D_MODEL, IN_WIDTH), lambda i: (0, 0), pipeline_mode=pl.Buffered(1)),
            pl.BlockSpec((1, IN_WIDTH), lambda i: (0, 0)),
            *rope_specs, *rope_specs,
            pl.BlockSpec((wout_rows, D_MODEL), lambda i: (i, 0)),
            pl.BlockSpec((wout_rows, 2 * D_FF), lambda i: (i, 0)),
        ],
        out_specs=[
            pl.BlockSpec((MID_DIL, tm // MID_DIL, QKV_A_WIDTH), lambda i: (0, i, 0)),
            pl.BlockSpec((FAR_DIL, tm // FAR_DIL, QKV_A_WIDTH), lambda i: (0, i, 0)),
            pl.BlockSpec((tm, QKV_B_WIDTH), lambda i: (i, 0)),
            pl.BlockSpec((wout_rows, D_MODEL), lambda i: (i, 0)),
            pl.BlockSpec((wout_rows, 2 * D_FF), lambda i: (i, 0)),
        ],
        out_shape=[
            jax.ShapeDtypeStruct((MID_DIL, seq // MID_DIL, QKV_A_WIDTH), BF16),
            jax.ShapeDtypeStruct((FAR_DIL, seq // FAR_DIL, QKV_A_WIDTH), BF16),
            jax.ShapeDtypeStruct((seq, QKV_B_WIDTH), BF16),
            jax.ShapeDtypeStruct((D_MODEL, D_MODEL), BF16),
            jax.ShapeDtypeStruct((D_MODEL, 2 * D_FF), BF16),
        ],
        scratch_shapes=[pltpu.VMEM((groups, tm, LANES), F32), pltpu.VMEM((groups, tm, LANES), F32)],
        compiler_params=_params(("arbitrary",), vmem),
        name="proj_rope",
    )(x2d, w_in.astype(BF16), b_in.reshape(1, IN_WIDTH), *rope_a, *rope_b, w_out, w_up)


def _band_masks(lo, is_first_tile):
    qi = lax.broadcasted_iota(jnp.int32, (BLOCK, 2 * BLOCK), 0)
    kj = lax.broadcasted_iota(jnp.int32, (BLOCK, 2 * BLOCK), 1)
    band = (kj >= qi + lo) & (kj <= qi + BLOCK)
    first_lower = jnp.where(is_first_tile, BLOCK, 0)
    return band, band & (kj >= first_lower)


def _near_masks(is_first_tile):
    iq = lax.broadcasted_iota(jnp.int32, (BLOCK, 2 * BLOCK), 0)
    ik = lax.broadcasted_iota(jnp.int32, (BLOCK, 2 * BLOCK), 1)
    cq, mq = iq // NEAR_SLAB, iq % NEAR_SLAB
    ck, mk = ik // (2 * NEAR_SLAB), ik % (2 * NEAR_SLAB)
    dist = MID_DIL * (mq + NEAR_SLAB - mk) + (cq - ck)
    valid = (dist >= 0) & (dist <= BLOCK)
    first_lower = jnp.where(is_first_tile, NEAR_SLAB, 0)
    return valid, valid & (mk >= first_lower)


def _rms_scale(sum_sq, width):
    return lax.rsqrt(sum_sq * (1.0 / width) + RMS_EPS)


def _scores(q, k, valid):
    s = lax.dot_general(q, k, (((1,), (1,)), ((), ())), preferred_element_type=F32)
    return jnp.where(valid, s, -jnp.inf)


def _attn_a_kernel(q4_ref, k4_ref, k4h_ref, v4_ref, v4h_ref,
                   q16_ref, k16_ref, k16h_ref, v16_ref, v16h_ref, y_ref,
                   k4buf, v4buf, k16buf, v16buf,
                   acc_n, m_n, l_n, acc_m, m_m, l_m, o_s, y_s):
    first = pl.program_id(0) == 0
    k4buf[:, 0:BLOCK, :] = k4h_ref[...]
    k4buf[:, BLOCK:, :] = k4_ref[...]
    v4buf[:, 0:BLOCK, :] = v4h_ref[...]
    v4buf[:, BLOCK:, :] = v4_ref[...]
    k16buf[:, 0:BLOCK, :] = k16h_ref[...]
    k16buf[:, BLOCK:, :] = k16_ref[...]
    v16buf[:, 0:BLOCK, :] = v16h_ref[...]
    v16buf[:, BLOCK:, :] = v16_ref[...]

    near, near_first = _near_masks(first)
    band, band_first = _band_masks(0, first)
    full = (BLOCK, LANES)
    ones = jnp.ones((2 * BLOCK, LANES), BF16)

    work = []

    def near_item(j):
        r0 = j * NEAR_SLAB
        h0 = BLOCK + r0 - NEAR_SLAB

        def gather(ref, start, size):
            return jnp.concatenate([ref[c, start:start + size, :] for c in range(MID_DIL)], axis=0)

        def store(m, l, acc):
            for c in range(MID_DIL):
                dst = slice(c * A_MID_ROWS + r0, c * A_MID_ROWS + r0 + NEAR_SLAB)
                src = slice(c * NEAR_SLAB, (c + 1) * NEAR_SLAB)
                acc_n[dst, :] = acc[src]
                m_n[dst, :] = m[src]
                l_n[dst, :] = l[src]

        return (lambda: gather(q4_ref, r0, NEAR_SLAB), lambda: gather(k4buf, h0, 2 * NEAR_SLAB),
                lambda: gather(v4buf, h0, 2 * NEAR_SLAB), near_first if j == 0 else near, None, store)

    def mid_item(c, j):
        rows = slice(c * A_MID_ROWS + j * BLOCK, c * A_MID_ROWS + (j + 1) * BLOCK)

        def store(m, l, acc):
            acc_m[rows, :] = acc
            m_m[rows, :] = m
            l_m[rows, :] = l

        return (lambda: q4_ref[c, j * BLOCK:(j + 1) * BLOCK, :],
                lambda: k4buf[c, j * BLOCK:(j + 2) * BLOCK, :],
                lambda: v4buf[c, j * BLOCK:(j + 2) * BLOCK, :],
                band_first if j == 0 else band,
                lambda: (m_n[rows, :], l_n[rows, :], acc_n[rows, :]), store)

    def far_item(c, c2):
        r = c + MID_DIL * c2
        rows = pl.ds(c * A_MID_ROWS + c2, BLOCK, stride=MID_DIL)

        def store(m, l, acc):
            o_s[rows, :] = acc * (1.0 / l)

        return (lambda: q16_ref[r], lambda: k16buf[r], lambda: v16buf[r], band_first,
                lambda: (m_m[rows, :], l_m[rows, :], acc_m[rows, :]), store)

    work += [near_item(j) for j in range(A_TILE // BLOCK)]
    work += [mid_item(c, j) for c in range(MID_DIL) for j in range(A_MID_ROWS // BLOCK)]
    work += [far_item(c, c2) for c in range(MID_DIL) for c2 in range(FAR_DIL // MID_DIL)]
    groups = [work[g:g + GROUP] for g in range(0, len(work), GROUP)]

    def score_group(items):
        return [_scores(q(), k(), valid) for q, k, _, valid, _, _ in items]

    def finish_group(items, scores):
        staged = []
        for (_, _, _, _, old, _), s in zip(items, scores):
            m = jnp.max(s, axis=1, keepdims=True)
            if old is None:
                staged.append((jnp.broadcast_to(m, full), jnp.exp2(s - m).astype(BF16), None))
                continue
            m_old, l_old, acc_old = old()
            m = jnp.maximum(m_old, m)
            p = jnp.concatenate([jnp.exp2(s[:, :LANES] - m), jnp.exp2(s[:, LANES:] - m)], axis=1)
            staged.append((m, p.astype(BF16), (jnp.exp2(m_old - m), l_old, acc_old)))
        for (_, _, v, _, _, store), (m, p, old) in zip(items, staged):
            pv = jnp.dot(p, jnp.concatenate([v(), ones], axis=1), preferred_element_type=F32)
            acc, l = pv[:, :LANES], pv[:, LANES:]
            if old is not None:
                rescale, l_old, acc_old = old
                acc = rescale * acc_old + acc
                l = rescale * l_old + l
            store(m, l, acc)

    scores = score_group(groups[0])
    for g, items in enumerate(groups):
        upcoming = score_group(groups[g + 1]) if g + 1 < len(groups) else None
        finish_group(items, scores)
        scores = upcoming

    for c in range(MID_DIL):
        y_s[pl.ds(c, A_MID_ROWS, stride=MID_DIL), :] = o_s[c * A_MID_ROWS:(c + 1) * A_MID_ROWS, :]
    y_ref[...] = y_s[...].astype(BF16)


def _attn_a_call(qkv4, qkv16):
    seq = qkv4.shape[0] * qkv4.shape[1]
    tiles = seq // A_TILE
    mid_blocks = A_MID_ROWS // BLOCK

    def mid_tile(which):
        return pl.BlockSpec((MID_DIL, A_MID_ROWS, A_HEAD_DIM),
                            lambda t, h: (0, t, which * A_HEADS + h))

    def mid_halo(which):
        return pl.BlockSpec((MID_DIL, BLOCK, A_HEAD_DIM),
                            lambda t, h: (0, jnp.maximum(t * mid_blocks - 1, 0), which * A_HEADS + h))

    def far_tile(which):
        return pl.BlockSpec((FAR_DIL, BLOCK, A_HEAD_DIM), lambda t, h: (0, t, which * A_HEADS + h))

    def far_halo(which):
        return pl.BlockSpec((FAR_DIL, BLOCK, A_HEAD_DIM),
                            lambda t, h: (0, jnp.maximum(t - 1, 0), which * A_HEADS + h))

    slab = A_TILE * A_HEAD_DIM
    vmem = (2 * (3 * slab + 2 * slab // 4) * 2
            + 2 * 5 * slab * 2
            + 2 * slab * 2
            + 2 * (slab + slab // 4) * 2 + 2 * 2 * slab * 2
            + 8 * slab * 4)
    return pl.pallas_call(
        _attn_a_kernel,
        grid=(tiles, A_HEADS),
        in_specs=[mid_tile(0), mid_tile(1), mid_halo(1), mid_tile(2), mid_halo(2),
                  far_tile(0), far_tile(1), far_halo(1), far_tile(2), far_halo(2)],
        out_specs=pl.BlockSpec((A_TILE, A_HEAD_DIM), lambda t, h: (t, h)),
        out_shape=jax.ShapeDtypeStruct((seq, A_WIDTH), BF16),
        scratch_shapes=[
            pltpu.VMEM((MID_DIL, A_MID_ROWS + BLOCK, A_HEAD_DIM), BF16),
            pltpu.VMEM((MID_DIL, A_MID_ROWS + BLOCK, A_HEAD_DIM), BF16),
            pltpu.VMEM((FAR_DIL, 2 * BLOCK, A_HEAD_DIM), BF16),
            pltpu.VMEM((FAR_DIL, 2 * BLOCK, A_HEAD_DIM), BF16),
        ] + [pltpu.VMEM((A_TILE, LANES), F32)] * 8,
        compiler_params=_params(("arbitrary", "arbitrary"), vmem),
        name="attn_a",
    )(qkv4, qkv4, qkv4, qkv4, qkv4, qkv16, qkv16, qkv16, qkv16, qkv16)


def _attn_b_kernel(sinks_ref, q_ref, km_ref, kh_ref, vm_ref, vh_ref, gain_ref, y_ref,
                   klo, khi, vlo, vhi, obuf, *, rows):
    lane = lax.broadcasted_iota(jnp.int32, (rows + BLOCK, LANES), 1)
    low = lane < B_HEAD_DIM

    def fill(dst_lo, dst_hi, halo_ref, main_ref):
        full = jnp.concatenate([halo_ref[...], main_ref[...]], axis=0).astype(F32)
        swapped = pltpu.roll(full, B_HEAD_DIM, 1)
        zero = jnp.zeros_like(full)
        for c in range(B_KV_HEADS):
            own_low, own_high = (full, swapped) if c == 0 else (swapped, full)
            dst_lo[c] = jnp.where(low, own_low, zero).astype(BF16)
            dst_hi[c] = jnp.where(low, zero, own_high).astype(BF16)

    fill(klo, khi, kh_ref, km_ref)
    fill(vlo, vhi, vh_ref, vm_ref)

    band, band_first = _band_masks(1, pl.program_id(0) == 0)

    items = [(j, pair) for j in range(rows // BLOCK) for pair in range(B_HEADS // 2)]
    groups = [items[g:g + B_ITEM_GROUP] for g in range(0, len(items), B_ITEM_GROUP)]

    col0 = lax.broadcasted_iota(jnp.int32, (1, LANES), 1) == 0
    neg_inf_row = jnp.full((1, LANES), -jnp.inf, F32)

    def score_group(group):
        out = []
        for j, pair in group:
            c = (2 * pair) // B_GROUP
            r0 = j * BLOCK
            q = q_ref[r0:r0 + BLOCK, pair * LANES:(pair + 1) * LANES]
            valid = band_first if j == 0 else band
            pair_scores = []
            for odd, k_sel in enumerate((klo, khi)):
                s = lax.dot_general(q, k_sel[c, r0:r0 + 2 * BLOCK, :], (((1,), (1,)), ((), ())),
                                    preferred_element_type=F32)
                sink_row = jnp.where(col0, sinks_ref[2 * pair + odd] * LOG2_E, neg_inf_row)
                pair_scores.append(jnp.concatenate(
                    [jnp.where(valid[:, :LANES], s[:, :LANES], sink_row),
                     jnp.where(valid[:, LANES:], s[:, LANES:], -jnp.inf)], axis=1))
            out.append(pair_scores)
        return out

    def finish_group(group, scores):
        staged = []
        for (j, pair), pair_scores in zip(group, scores):
            for odd, s in enumerate(pair_scores):
                m = jnp.max(s, axis=1, keepdims=True)
                p = jnp.exp2(s - m)
                l = jnp.sum(p, axis=1, keepdims=True)
                p = jnp.concatenate([jnp.where(col0, 0.0, p[:, :LANES]), p[:, LANES:]], axis=1)
                staged.append((p.astype(BF16), 1.0 / l))
        for i, (j, pair) in enumerate(group):
            c = (2 * pair) // B_GROUP
            r0 = j * BLOCK
            o = jnp.zeros((BLOCK, LANES), F32)
            for odd, v_sel in enumerate((vlo, vhi)):
                p, weight = staged[2 * i + odd]
                acc = jnp.dot(p, v_sel[c, r0:r0 + 2 * BLOCK, :], preferred_element_type=F32)
                o = o + acc * weight
            obuf[r0:r0 + BLOCK, pair * LANES:(pair + 1) * LANES] = o

    scores = score_group(groups[0])
    for g, group in enumerate(groups):
        upcoming = score_group(groups[g + 1]) if g + 1 < len(groups) else None
        finish_group(group, scores)
        scores = upcoming

    for j in range(rows // BLOCK):
        o = obuf[j * BLOCK:(j + 1) * BLOCK, :]
        scale = _rms_scale(jnp.sum(o * o, axis=1, keepdims=True), B_WIDTH)
        y_ref[j * BLOCK:(j + 1) * BLOCK, :] = (o * scale * gain_ref[...]).astype(BF16)


def _attn_b_call(qkv_b, sinks, gain):
    seq = qkv_b.shape[0]
    rows = B_ROWS
    blocks_per_tile = rows // BLOCK
    k_col = B_WIDTH // B_KV_WIDTH
    v_col = k_col + 1

    def kv_tile(col):
        return pl.BlockSpec((rows, B_KV_WIDTH), lambda t: (t, col))

    def kv_halo(col):
        return pl.BlockSpec((BLOCK, B_KV_WIDTH),
                            lambda t: (jnp.maximum(t * blocks_per_tile - 1, 0), col))

    kv_scratch = pltpu.VMEM((B_KV_HEADS, rows + BLOCK, LANES), BF16)
    vmem = (2 * 2 * rows * B_WIDTH * 2 + 4 * 2 * (rows + BLOCK) * LANES * 2
            + 4 * B_KV_HEADS * (rows + BLOCK) * LANES * 2 + rows * B_WIDTH * 4
            + 4 * (rows + BLOCK) * LANES * 4)
    return pl.pallas_call(
        functools.partial(_attn_b_kernel, rows=rows),
        grid=(seq // rows,),
        in_specs=[
            pl.BlockSpec(memory_space=pltpu.SMEM),
            pl.BlockSpec((rows, B_WIDTH), lambda t: (t, 0)),
            kv_tile(k_col), kv_halo(k_col), kv_tile(v_col), kv_halo(v_col),
            pl.BlockSpec((1, B_WIDTH), lambda t: (0, 0)),
        ],
        out_specs=pl.BlockSpec((rows, B_WIDTH), lambda t: (t, 0)),
        out_shape=jax.ShapeDtypeStruct((seq, B_WIDTH), BF16),
        scratch_shapes=[kv_scratch, kv_scratch, kv_scratch, kv_scratch,
                        pltpu.VMEM((rows, B_WIDTH), F32)],
        compiler_params=_params(("arbitrary",), vmem),
        name="attn_b",
    )(sinks, qkv_b, qkv_b, qkv_b, qkv_b, qkv_b, gain.reshape(1, B_WIDTH))


def _layer_norm_rows(z, g, b):
    mu = jnp.mean(z, axis=-1, keepdims=True)
    zc = z - mu
    var = jnp.mean(zc * zc, axis=-1, keepdims=True)
    return zc * lax.rsqrt(var + LN_EPS) * g + b


def _out_kernel(ya_ref, yb_ref, ga_ref, w_ref, x_ref, g_ref, b_ref, wdn_ref,
                h_ref, wdn_bf_ref):
    wdn_bf_ref[...] = wdn_ref[...].astype(BF16)
    blocks = [slice(r, r + OUT_SUB_ROWS) for r in range(0, ya_ref.shape[0], OUT_SUB_ROWS)]
    mixes = []
    for rows in blocks:
        ya = ya_ref[rows, :].astype(F32)
        scale = _rms_scale(jnp.sum(ya * ya, axis=1, keepdims=True), A_WIDTH)
        ya = (ya * scale * ga_ref[...]).astype(BF16)
        mix = jnp.dot(ya, w_ref[0:A_WIDTH, :], preferred_element_type=F32)
        mixes.append(mix + jnp.dot(yb_ref[rows, :], w_ref[A_WIDTH:, :], preferred_element_type=F32))
    for rows, mix in zip(blocks, mixes):
        h_ref[rows, :] = _layer_norm_rows(ALPHA * x_ref[rows, :] + mix, g_ref[...], b_ref[...])


def _out_call(ya, yb, gain_a, w_out_bf, x2d, ln_g, ln_b, w_down):
    seq = x2d.shape[0]
    tm = OUT_ROWS
    steps = seq // tm
    down_rows = D_FF // steps
    row_spec = pl.BlockSpec((1, D_MODEL), lambda i: (0, 0))
    vmem = (D_MODEL * D_MODEL * 2 + 2 * 2 * tm * A_WIDTH * 2
            + 2 * 2 * tm * D_MODEL * 4 + 2 * tm * D_MODEL * 4
            + 2 * down_rows * D_MODEL * 6)
    return pl.pallas_call(
        _out_kernel,
        grid=(steps,),
        in_specs=[
            pl.BlockSpec((tm, A_WIDTH), lambda i: (i, 0)),
            pl.BlockSpec((tm, B_WIDTH), lambda i: (i, 0)),
            pl.BlockSpec((1, A_WIDTH), lambda i: (0, 0)),
            pl.BlockSpec((D_MODEL, D_MODEL), lambda i: (0, 0), pipeline_mode=pl.Buffered(1)),
            pl.BlockSpec((tm, D_MODEL), lambda i: (i, 0)),
            row_spec, row_spec,
            pl.BlockSpec((down_rows, D_MODEL), lambda i: (i, 0)),
        ],
        out_specs=[
            pl.BlockSpec((tm, D_MODEL), lambda i: (i, 0)),
            pl.BlockSpec((down_rows, D_MODEL), lambda i: (i, 0)),
        ],
        out_shape=[
            jax.ShapeDtypeStruct((seq, D_MODEL), F32),
            jax.ShapeDtypeStruct((D_FF, D_MODEL), BF16),
        ],
        compiler_params=_params(("arbitrary",), vmem),
        name="out_proj_ln",
    )(ya, yb, gain_a.reshape(1, A_WIDTH), w_out_bf, x2d,
      ln_g.reshape(1, D_MODEL), ln_b.reshape(1, D_MODEL), w_down)


def _ffn_kernel(h_ref, wg_ref, wu_ref, wd_ref, g_ref, b_ref, o_ref, hb_ref):
    j = pl.program_id(1)
    last = pl.num_programs(1) - 1
    half = wg_ref.shape[1] // 2
    halves = [slice(0, half), slice(half, 2 * half)]

    def activations(hb):
        gates = [jnp.dot(hb, wg_ref[:, s], preferred_element_type=F32) for s in halves]
        ups = [jnp.dot(hb, wu_ref[:, s], preferred_element_type=F32) for s in halves]
        return [(g * jax.nn.sigmoid(g) * u).astype(BF16) for g, u in zip(gates, ups)]

    def down(acts, rows=slice(None)):
        return (jnp.dot(acts[0][rows], wd_ref[halves[0], :], preferred_element_type=F32)
                + jnp.dot(acts[1][rows], wd_ref[halves[1], :], preferred_element_type=F32))

    @pl.when(j == 0)
    def _():
        h = h_ref[...]
        hb = h.astype(BF16)
        hb_ref[...] = hb
        o_ref[...] = ALPHA * h + down(activations(hb))

    @pl.when((j > 0) & (j < last))
    def _():
        o_ref[...] += down(activations(hb_ref[...]))

    @pl.when(j == last)
    def _():
        acts = activations(hb_ref[...])
        for r in range(0, o_ref.shape[0], FFN_NORM_ROWS):
            rows = slice(r, r + FFN_NORM_ROWS)
            o_ref[rows, :] = _layer_norm_rows(o_ref[rows, :] + down(acts, rows),
                                              g_ref[...], b_ref[...])


def _ffn_call(h, w_up_b, w_down_b, ln_g, ln_b):
    seq = h.shape[0]
    tm, tf = FFN_ROWS, FFN_CHUNK
    n_chunks = D_FF // tf
    row_spec = pl.BlockSpec((1, D_MODEL), lambda i, j: (0, 0))
    vmem = (2 * tm * D_MODEL * 4 + tm * D_MODEL * 2 + 2 * tm * D_MODEL * 4
            + 2 * 3 * D_MODEL * tf * 2 + 3 * tm * tf * 4)
    return pl.pallas_call(
        _ffn_kernel,
        grid=(seq // tm, n_chunks),
        in_specs=[
            pl.BlockSpec((tm, D_MODEL), lambda i, j: (i, 0)),
            pl.BlockSpec((D_MODEL, tf), lambda i, j: (0, j)),
            pl.BlockSpec((D_MODEL, tf), lambda i, j: (0, n_chunks + j)),
            pl.BlockSpec((tf, D_MODEL), lambda i, j: (j, 0)),
            row_spec, row_spec,
        ],
        out_specs=pl.BlockSpec((tm, D_MODEL), lambda i, j: (i, 0)),
        out_shape=jax.ShapeDtypeStruct((seq, D_MODEL), F32),
        scratch_shapes=[pltpu.VMEM((tm, D_MODEL), BF16)],
        compiler_params=_params(("arbitrary", "arbitrary"), vmem),
        name="swiglu_ln",
    )(h, w_up_b, w_up_b, w_down_b, ln_g.reshape(1, D_MODEL), ln_b.reshape(1, D_MODEL))


def _layer(x2d, w_in, b_in, sinks, g_mix_a, g_mix_b, w_out, ln1_g, ln1_b, w_up, w_down, ln2_g, ln2_b):
    qkv4, qkv16, qkv_b, w_out_b, w_up_b = _proj_call(x2d, w_in, b_in, w_out, w_up)
    ya = _attn_a_call(qkv4, qkv16)
    yb = _attn_b_call(qkv_b, sinks, g_mix_b)
    h, w_down_b = _out_call(ya, yb, g_mix_a, w_out_b, x2d, ln1_g, ln1_b, w_down)
    return _ffn_call(h, w_up_b, w_down_b, ln2_g, ln2_b)


def kernel(x, w_in, b_in, sinks, g_mix_a, g_mix_b, w_out, ln1_g, ln1_b, w_up, w_down, ln2_g, ln2_b):
    batch, seq, d_model = x.shape
    assert (batch, d_model) == (1, D_MODEL) and seq % A_TILE == 0
    h = x.reshape(seq, d_model)
    for i in range(w_in.shape[0]):
        h = _layer(h, w_in[i], b_in[i], sinks[i], g_mix_a[i], g_mix_b[i], w_out[i],
                   ln1_g[i], ln1_b[i], w_up[i], w_down[i], ln2_g[i], ln2_b[i])
    return h.reshape(batch, seq, d_model)
```

```python
import functools
import math

import jax
import jax.numpy as jnp
from jax import lax
from jax.experimental import pallas as pl
from jax.experimental.pallas import tpu as pltpu

F32 = jnp.float32
BF16 = jnp.bfloat16

D_MODEL = 2048
A_HEAD_DIM = 128
A_WIDTH = 1024
A_HEADS = 8
MID_DIL = 4
FAR_DIL = 16
B_HEAD_DIM = 64
B_WIDTH = 1024
B_HEADS = 16
B_KV_HEADS = 2
B_GROUP = 8
B_KV_WIDTH = 128
QKV_A_WIDTH = 3 * A_WIDTH
QKV_B_WIDTH = B_WIDTH + 2 * B_KV_WIDTH
IN_WIDTH = QKV_A_WIDTH + QKV_B_WIDTH
D_FF = 5632
BLOCK = 128
ROPE_THETA = 10000.0
ALPHA = 2.0 ** 0.25
LN_EPS = 1e-5
RMS_EPS = 1e-6
LOG2_E = math.log2(math.e)

LANES = 128
V7X_VMEM_BYTES = 64 * 1024 * 1024
V7X_VMEM_REQUEST_CAP = V7X_VMEM_BYTES - 2 * 1024 * 1024

PROJ_ROWS = 512
PROJ_CHUNK = 512
A_TILE = FAR_DIL * BLOCK
B_ROWS = 512
OUT_ROWS = 512
OUT_SUB_ROWS = 256
FFN_ROWS = 1024
FFN_CHUNK = 512
FFN_NORM_ROWS = 256

A_MID_ROWS = A_TILE // MID_DIL
NEAR_SLAB = BLOCK // MID_DIL
GROUP = 4
GROUPS_AHEAD = 1
B_ITEM_GROUP = 2


def _vmem_limit(estimate_bytes, result_bytes=0):
    request = estimate_bytes + estimate_bytes // 4
    if result_bytes:
        request = max(request, V7X_VMEM_BYTES - result_bytes + result_bytes // 8)
    return int(min(V7X_VMEM_REQUEST_CAP, request))


def _params(semantics, vmem_estimate, result_bytes=0):
    return pltpu.CompilerParams(dimension_semantics=semantics,
                                vmem_limit_bytes=_vmem_limit(vmem_estimate, result_bytes))


def _rope_factors(seq, tile, head_dim):
    half = head_dim // 2
    inv_freq = ROPE_THETA ** (-jnp.arange(half, dtype=F32) / half)
    reps = LANES // head_dim

    def expand(pos):
        ang = pos.astype(F32)[:, None] * inv_freq[None, :]
        cos = jnp.cos(ang)
        sin = jnp.sin(ang)
        return (jnp.tile(jnp.concatenate([cos, cos], axis=-1), (1, reps)),
                jnp.tile(jnp.concatenate([sin, sin], axis=-1), (1, reps)))

    return expand(jnp.arange(tile)) + expand(jnp.arange(seq // tile) * tile)


def _proj_kernel(x_ref, w_ref, b_ref, oca_ref, osa_ref, tca_ref, tsa_ref,
                 ocb_ref, osb_ref, tcb_ref, tsb_ref, wout_ref, wup_ref,
                 o4_ref, o16_ref, ob_ref, wout_bf_ref, wup_bf_ref, nat_buf, mid_buf):
    rows = x_ref.shape[0]
    step = pl.program_id(0)
    wout_bf_ref[...] = wout_ref[...].astype(BF16)
    wup_bf_ref[...] = wup_ref[...].astype(BF16)
    xb = x_ref[...].astype(BF16)
    lane = lax.broadcasted_iota(jnp.int32, (rows, LANES), 1)
    first_half_a = lane < (A_HEAD_DIM // 2)
    first_half_b = (lane % B_HEAD_DIM) < (B_HEAD_DIM // 2)

    def tables(off_c_ref, off_s_ref, start_c_ref, start_s_ref, first_half):
        oc, os_ = off_c_ref[...], off_s_ref[...]
        tc, ts = start_c_ref[pl.ds(step, 1), :], start_s_ref[pl.ds(step, 1), :]
        sin = ts * oc + tc * os_
        return tc * oc - ts * os_, jnp.where(first_half, -sin, sin)

    ca, sa = tables(oca_ref, osa_ref, tca_ref, tsa_ref, first_half_a)
    cb, sb = tables(ocb_ref, osb_ref, tcb_ref, tsb_ref, first_half_b)
    scale_a = LOG2_E / math.sqrt(A_HEAD_DIM)
    scale_b = LOG2_E / math.sqrt(B_HEAD_DIM)
    mid_rows = rows // MID_DIL
    far_rows = rows // FAR_DIL

    def rope_a(y):
        return y * ca + pltpu.roll(y, A_HEAD_DIM // 2, 1) * sa

    def rope_b(y):
        partner = jnp.where(first_half_b,
                            pltpu.roll(y, LANES - B_HEAD_DIM // 2, 1),
                            pltpu.roll(y, B_HEAD_DIM // 2, 1))
        return y * cb + partner * sb

    for c0 in range(0, IN_WIDTH, PROJ_CHUNK):
        c1 = min(c0 + PROJ_CHUNK, IN_WIDTH)
        acc = jnp.dot(xb, w_ref[:, c0:c1], preferred_element_type=F32) + b_ref[:, c0:c1]
        for g, col in enumerate(range(c0, c1, LANES)):
            y = acc[:, col - c0:col - c0 + LANES]
            if col < A_WIDTH:
                y = rope_a(y) * scale_a
            elif col < 2 * A_WIDTH:
                y = rope_a(y)
            elif col < QKV_A_WIDTH:
                pass
            elif col < QKV_A_WIDTH + B_WIDTH:
                y = rope_b(y) * scale_b
            elif col < QKV_A_WIDTH + B_WIDTH + B_KV_WIDTH:
                y = rope_b(y)
            if col >= QKV_A_WIDTH:
                ob_ref[:, col - QKV_A_WIDTH:col - QKV_A_WIDTH + LANES] = y.astype(BF16)
                continue
            nat_buf[g] = y
            for c in range(MID_DIL):
                cls = nat_buf[g, pl.ds(c, mid_rows, stride=MID_DIL), :]
                o4_ref[c, :, col:col + LANES] = cls.astype(BF16)
                mid_buf[g, c * mid_rows:(c + 1) * mid_rows, :] = cls
            for c in range(MID_DIL):
                for c2 in range(FAR_DIL // MID_DIL):
                    cls = mid_buf[g, pl.ds(c * mid_rows + c2, far_rows, stride=MID_DIL), :]
                    o16_ref[c + MID_DIL * c2, :, col:col + LANES] = cls.astype(BF16)


def _proj_call(x2d, w_in, b_in, w_out, w_up):
    seq = x2d.shape[0]
    tm = PROJ_ROWS
    steps = seq // tm
    rope_a = _rope_factors(seq, tm, A_HEAD_DIM)
    rope_b = _rope_factors(seq, tm, B_HEAD_DIM)
    offset_spec = pl.BlockSpec((tm, LANES), lambda i: (0, 0))
    start_spec = pl.BlockSpec((steps, LANES), lambda i: (0, 0))
    rope_specs = [offset_spec, offset_spec, start_spec, start_spec]
    wout_rows = D_MODEL // steps
    groups = PROJ_CHUNK // LANES
    vmem = (D_MODEL * IN_WIDTH * 2
            + 2 * tm * D_MODEL * 4
            + 2 * tm * (2 * QKV_A_WIDTH + QKV_B_WIDTH) * 2
            + 2 * 4 * (tm + steps) * LANES * 4
            + 2 * wout_rows * (D_MODEL + 2 * D_FF) * 6
            + 2 * groups * tm * LANES * 4
            + tm * D_MODEL * 2 + 2 * tm * PROJ_CHUNK * 4)
    return pl.pallas_call(
        _proj_kernel,
        grid=(steps,),
        in_specs=[
            pl.BlockSpec((tm, D_MODEL), lambda i: (i, 0)),
            pl.BlockSpec((D_MODEL, IN_WIDTH), lambda i: (0, 0), pipeline_mode=pl.Buffered(1)),
            pl.BlockSpec((1, IN_WIDTH), lambda i: (0, 0)),
            *rope_specs, *rope_specs,
            pl.BlockSpec((wout_rows, D_MODEL), lambda i: (i, 0)),
            pl.BlockSpec((wout_rows, 2 * D_FF), lambda i: (i, 0)),
        ],
        out_specs=[
            pl.BlockSpec((MID_DIL, tm // MID_DIL, QKV_A_WIDTH), lambda i: (0, i, 0)),
            pl.BlockSpec((FAR_DIL, tm // FAR_DIL, QKV_A_WIDTH), lambda i: (0, i, 0)),
            pl.BlockSpec((tm, QKV_B_WIDTH), lambda i: (i, 0)),
            pl.BlockSpec((wout_rows, D_MODEL), lambda i: (i, 0)),
            pl.BlockSpec((wout_rows, 2 * D_FF), lambda i: (i, 0)),
        ],
        out_shape=[
            jax.ShapeDtypeStruct((MID_DIL, seq // MID_DIL, QKV_A_WIDTH), BF16),
            jax.ShapeDtypeStruct((FAR_DIL, seq // FAR_DIL, QKV_A_WIDTH), BF16),
            jax.ShapeDtypeStruct((seq, QKV_B_WIDTH), BF16),
            jax.ShapeDtypeStruct((D_MODEL, D_MODEL), BF16),
            jax.ShapeDtypeStruct((D_MODEL, 2 * D_FF), BF16),
        ],
        scratch_shapes=[pltpu.VMEM((groups, tm, LANES), F32), pltpu.VMEM((groups, tm, LANES), F32)],
        compiler_params=_params(("arbitrary",), vmem),
        name="proj_rope",
    )(x2d, w_in.astype(BF16), b_in.reshape(1, IN_WIDTH), *rope_a, *rope_b, w_out, w_up)


def _band_masks(lo, is_first_tile):
    qi = lax.broadcasted_iota(jnp.int32, (BLOCK, 2 * BLOCK), 0)
    kj = lax.broadcasted_iota(jnp.int32, (BLOCK, 2 * BLOCK), 1)
    band = (kj >= qi + lo) & (kj <= qi + BLOCK)
    first_lower = jnp.where(is_first_tile, BLOCK, 0)
    return band, band & (kj >= first_lower)


def _near_masks(is_first_tile):
    iq = lax.broadcasted_iota(jnp.int32, (BLOCK, 2 * BLOCK), 0)
    ik = lax.broadcasted_iota(jnp.int32, (BLOCK, 2 * BLOCK), 1)
    cq, mq = iq // NEAR_SLAB, iq % NEAR_SLAB
    ck, mk = ik // (2 * NEAR_SLAB), ik % (2 * NEAR_SLAB)
    dist = MID_DIL * (mq + NEAR_SLAB - mk) + (cq - ck)
    valid = (dist >= 0) & (dist <= BLOCK)
    first_lower = jnp.where(is_first_tile, NEAR_SLAB, 0)
    return valid, valid & (mk >= first_lower)


def _rms_scale(sum_sq, width):
    return lax.rsqrt(sum_sq * (1.0 / width) + RMS_EPS)


def _qk(q, k):
    return lax.dot_general(q, k, (((1,), (1,)), ((), ())), preferred_element_type=F32)


def _attn_a_kernel(q4_ref, k4_ref, k4h_ref, v4_ref, v4h_ref,
                   q16_ref, k16_ref, k16h_ref, v16_ref, v16h_ref, y_ref,
                   k4buf, v4buf, k16buf, v16buf,
                   acc_n, m_n, l_n, acc_m, m_m, l_m, o_s, y_s):
    first = pl.program_id(0) == 0
    k4buf[:, 0:BLOCK, :] = k4h_ref[...]
    k4buf[:, BLOCK:, :] = k4_ref[...]
    v4buf[:, 0:BLOCK, :] = v4h_ref[...]
    v4buf[:, BLOCK:, :] = v4_ref[...]
    k16buf[:, 0:BLOCK, :] = k16h_ref[...]
    k16buf[:, BLOCK:, :] = k16_ref[...]
    v16buf[:, 0:BLOCK, :] = v16h_ref[...]
    v16buf[:, BLOCK:, :] = v16_ref[...]

    near, near_first = _near_masks(first)
    band, band_first = _band_masks(0, first)
    full = (BLOCK, LANES)
    ones = jnp.ones((2 * BLOCK, LANES), BF16)

    work = []

    def near_item(j):
        r0 = j * NEAR_SLAB
        h0 = BLOCK + r0 - NEAR_SLAB

        def gather(ref, start, size):
            return jnp.concatenate([ref[c, start:start + size, :] for c in range(MID_DIL)], axis=0)

        def store(m, l, acc):
            for c in range(MID_DIL):
                dst = slice(c * A_MID_ROWS + r0, c * A_MID_ROWS + r0 + NEAR_SLAB)
                src = slice(c * NEAR_SLAB, (c + 1) * NEAR_SLAB)
                acc_n[dst, :] = acc[src]
                m_n[dst, :] = m[src]
                l_n[dst, :] = l[src]

        return (lambda: gather(q4_ref, r0, NEAR_SLAB), lambda: gather(k4buf, h0, 2 * NEAR_SLAB),
                lambda: gather(v4buf, h0, 2 * NEAR_SLAB), near_first if j == 0 else near, None, store)

    def mid_item(c, j):
        rows = slice(c * A_MID_ROWS + j * BLOCK, c * A_MID_ROWS + (j + 1) * BLOCK)

        def store(m, l, acc):
            acc_m[rows, :] = acc
            m_m[rows, :] = m
            l_m[rows, :] = l

        return (lambda: q4_ref[c, j * BLOCK:(j + 1) * BLOCK, :],
                lambda: k4buf[c, j * BLOCK:(j + 2) * BLOCK, :],
                lambda: v4buf[c, j * BLOCK:(j + 2) * BLOCK, :],
                band_first if j == 0 else band,
                lambda: (m_n[rows, :], l_n[rows, :], acc_n[rows, :]), store)

    def far_item(c, c2):
        r = c + MID_DIL * c2
        rows = pl.ds(c * A_MID_ROWS + c2, BLOCK, stride=MID_DIL)

        def store(m, l, acc):
            o_s[rows, :] = acc * (1.0 / l)

        return (lambda: q16_ref[r], lambda: k16buf[r], lambda: v16buf[r], band_first,
                lambda: (m_m[rows, :], l_m[rows, :], acc_m[rows, :]), store)

    work += [near_item(j) for j in range(A_TILE // BLOCK)]
    work += [mid_item(c, j) for c in range(MID_DIL) for j in range(A_MID_ROWS // BLOCK)]
    work += [far_item(c, c2) for c in range(MID_DIL) for c2 in range(FAR_DIL // MID_DIL)]
    groups = [work[g:g + GROUP] for g in range(0, len(work), GROUP)]

    def score_group(items):
        return [_qk(q(), k()) for q, k, _, _, _, _ in items]

    def finish_group(items, scores):
        staged = []
        for (_, _, _, valid, old, _), s in zip(items, scores):
            s = jnp.where(valid, s, -jnp.inf)
            m = jnp.max(s, axis=1, keepdims=True)
            if old is None:
                staged.append((jnp.broadcast_to(m, full), jnp.exp2(s - m).astype(BF16), None))
                continue
            m_old, l_old, acc_old = old()
            m = jnp.maximum(m_old, m)
            p = jnp.concatenate([jnp.exp2(s[:, :LANES] - m), jnp.exp2(s[:, LANES:] - m)], axis=1)
            staged.append((m, p.astype(BF16), (jnp.exp2(m_old - m), l_old, acc_old)))
        for (_, _, v, _, _, store), (m, p, old) in zip(items, staged):
            pv = jnp.dot(p, jnp.concatenate([v(), ones], axis=1), preferred_element_type=F32)
            acc, l = pv[:, :LANES], pv[:, LANES:]
            if old is not None:
                rescale, l_old, acc_old = old
                acc = rescale * acc_old + acc
                l = rescale * l_old + l
            store(m, l, acc)

    pending = [score_group(items) for items in groups[:GROUPS_AHEAD]]
    for g, items in enumerate(groups):
        if g + GROUPS_AHEAD < len(groups):
            pending.append(score_group(groups[g + GROUPS_AHEAD]))
        finish_group(items, pending.pop(0))

    for c in range(MID_DIL):
        y_s[pl.ds(c, A_MID_ROWS, stride=MID_DIL), :] = o_s[c * A_MID_ROWS:(c + 1) * A_MID_ROWS, :]
    y_ref[...] = y_s[...].astype(BF16)


def _attn_a_call(qkv4, qkv16):
    seq = qkv4.shape[0] * qkv4.shape[1]
    tiles = seq // A_TILE
    mid_blocks = A_MID_ROWS // BLOCK

    def mid_tile(which):
        return pl.BlockSpec((MID_DIL, A_MID_ROWS, A_HEAD_DIM),
                            lambda t, h: (0, t, which * A_HEADS + h))

    def mid_halo(which):
        return pl.BlockSpec((MID_DIL, BLOCK, A_HEAD_DIM),
                            lambda t, h: (0, jnp.maximum(t * mid_blocks - 1, 0), which * A_HEADS + h))

    def far_tile(which):
        return pl.BlockSpec((FAR_DIL, BLOCK, A_HEAD_DIM), lambda t, h: (0, t, which * A_HEADS + h))

    def far_halo(which):
        return pl.BlockSpec((FAR_DIL, BLOCK, A_HEAD_DIM),
                            lambda t, h: (0, jnp.maximum(t - 1, 0), which * A_HEADS + h))

    slab = A_TILE * A_HEAD_DIM
    vmem = (2 * (3 * slab + 2 * slab // 4) * 2
            + 2 * 5 * slab * 2
            + 2 * slab * 2
            + 2 * (slab + slab // 4) * 2 + 2 * 2 * slab * 2
            + 8 * slab * 4)
    return pl.pallas_call(
        _attn_a_kernel,
        grid=(tiles, A_HEADS),
        in_specs=[mid_tile(0), mid_tile(1), mid_halo(1), mid_tile(2), mid_halo(2),
                  far_tile(0), far_tile(1), far_halo(1), far_tile(2), far_halo(2)],
        out_specs=pl.BlockSpec((A_TILE, A_HEAD_DIM), lambda t, h: (t, h)),
        out_shape=jax.ShapeDtypeStruct((seq, A_WIDTH), BF16),
        scratch_shapes=[
            pltpu.VMEM((MID_DIL, A_MID_ROWS + BLOCK, A_HEAD_DIM), BF16),
            pltpu.VMEM((MID_DIL, A_MID_ROWS + BLOCK, A_HEAD_DIM), BF16),
            pltpu.VMEM((FAR_DIL, 2 * BLOCK, A_HEAD_DIM), BF16),
            pltpu.VMEM((FAR_DIL, 2 * BLOCK, A_HEAD_DIM), BF16),
        ] + [pltpu.VMEM((A_TILE, LANES), F32)] * 8,
        compiler_params=_params(("arbitrary", "arbitrary"), vmem, result_bytes=seq * A_WIDTH * 2),
        name="attn_a",
    )(qkv4, qkv4, qkv4, qkv4, qkv4, qkv16, qkv16, qkv16, qkv16, qkv16)


def _attn_b_kernel(sinks_ref, q_ref, km_ref, kh_ref, vm_ref, vh_ref, gain_ref, y_ref,
                   klo, khi, vlo, vhi, obuf, *, rows):
    lane = lax.broadcasted_iota(jnp.int32, (rows + BLOCK, LANES), 1)
    low = lane < B_HEAD_DIM

    def fill(dst_lo, dst_hi, halo_ref, main_ref):
        full = jnp.concatenate([halo_ref[...], main_ref[...]], axis=0).astype(F32)
        swapped = pltpu.roll(full, B_HEAD_DIM, 1)
        zero = jnp.zeros_like(full)
        for c in range(B_KV_HEADS):
            own_low, own_high = (full, swapped) if c == 0 else (swapped, full)
            dst_lo[c] = jnp.where(low, own_low, zero).astype(BF16)
            dst_hi[c] = jnp.where(low, zero, own_high).astype(BF16)

    fill(klo, khi, kh_ref, km_ref)
    fill(vlo, vhi, vh_ref, vm_ref)

    band, band_first = _band_masks(1, pl.program_id(0) == 0)

    items = [(j, pair) for j in range(rows // BLOCK) for pair in range(B_HEADS // 2)]
    groups = [items[g:g + B_ITEM_GROUP] for g in range(0, len(items), B_ITEM_GROUP)]

    col0 = lax.broadcasted_iota(jnp.int32, (1, LANES), 1) == 0
    neg_inf_row = jnp.full((1, LANES), -jnp.inf, F32)

    def score_group(group):
        out = []
        for j, pair in group:
            c = (2 * pair) // B_GROUP
            r0 = j * BLOCK
            q = q_ref[r0:r0 + BLOCK, pair * LANES:(pair + 1) * LANES]
            valid = band_first if j == 0 else band
            pair_scores = []
            for odd, k_sel in enumerate((klo, khi)):
                s = _qk(q, k_sel[c, r0:r0 + 2 * BLOCK, :])
                sink_row = jnp.where(col0, sinks_ref[2 * pair + odd] * LOG2_E, neg_inf_row)
                pair_scores.append(jnp.concatenate(
                    [jnp.where(valid[:, :LANES], s[:, :LANES], sink_row),
                     jnp.where(valid[:, LANES:], s[:, LANES:], -jnp.inf)], axis=1))
            out.append(pair_scores)
        return out

    def finish_group(group, scores):
        staged = []
        for (j, pair), pair_scores in zip(group, scores):
            for odd, s in enumerate(pair_scores):
                m = jnp.max(s, axis=1, keepdims=True)
                p = jnp.exp2(s - m)
                l = jnp.sum(p, axis=1, keepdims=True)
                p = jnp.concatenate([jnp.where(col0, 0.0, p[:, :LANES]), p[:, LANES:]], axis=1)
                staged.append((p.astype(BF16), 1.0 / l))
        for i, (j, pair) in enumerate(group):
            c = (2 * pair) // B_GROUP
            r0 = j * BLOCK
            o = jnp.zeros((BLOCK, LANES), F32)
            for odd, v_sel in enumerate((vlo, vhi)):
                p, weight = staged[2 * i + odd]
                acc = jnp.dot(p, v_sel[c, r0:r0 + 2 * BLOCK, :], preferred_element_type=F32)
                o = o + acc * weight
            obuf[r0:r0 + BLOCK, pair * LANES:(pair + 1) * LANES] = o

    scores = score_group(groups[0])
    for g, group in enumerate(groups):
        upcoming = score_group(groups[g + 1]) if g + 1 < len(groups) else None
        finish_group(group, scores)
        scores = upcoming

    for j in range(rows // BLOCK):
        o = obuf[j * BLOCK:(j + 1) * BLOCK, :]
        scale = _rms_scale(jnp.sum(o * o, axis=1, keepdims=True), B_WIDTH)
        y_ref[j * BLOCK:(j + 1) * BLOCK, :] = (o * scale * gain_ref[...]).astype(BF16)


def _attn_b_call(qkv_b, sinks, gain):
    seq = qkv_b.shape[0]
    rows = B_ROWS
    blocks_per_tile = rows // BLOCK
    k_col = B_WIDTH // B_KV_WIDTH
    v_col = k_col + 1

    def kv_tile(col):
        return pl.BlockSpec((rows, B_KV_WIDTH), lambda t: (t, col))

    def kv_halo(col):
        return pl.BlockSpec((BLOCK, B_KV_WIDTH),
                            lambda t: (jnp.maximum(t * blocks_per_tile - 1, 0), col))

    kv_scratch = pltpu.VMEM((B_KV_HEADS, rows + BLOCK, LANES), BF16)
    vmem = (2 * 2 * rows * B_WIDTH * 2 + 4 * 2 * (rows + BLOCK) * LANES * 2
            + 4 * B_KV_HEADS * (rows + BLOCK) * LANES * 2 + rows * B_WIDTH * 4
            + 4 * (rows + BLOCK) * LANES * 4)
    return pl.pallas_call(
        functools.partial(_attn_b_kernel, rows=rows),
        grid=(seq // rows,),
        in_specs=[
            pl.BlockSpec(memory_space=pltpu.SMEM),
            pl.BlockSpec((rows, B_WIDTH), lambda t: (t, 0)),
            kv_tile(k_col), kv_halo(k_col), kv_tile(v_col), kv_halo(v_col),
            pl.BlockSpec((1, B_WIDTH), lambda t: (0, 0)),
        ],
        out_specs=pl.BlockSpec((rows, B_WIDTH), lambda t: (t, 0)),
        out_shape=jax.ShapeDtypeStruct((seq, B_WIDTH), BF16),
        scratch_shapes=[kv_scratch, kv_scratch, kv_scratch, kv_scratch,
                        pltpu.VMEM((rows, B_WIDTH), F32)],
        compiler_params=_params(("arbitrary",), vmem, result_bytes=seq * B_WIDTH * 2),
        name="attn_b",
    )(sinks, qkv_b, qkv_b, qkv_b, qkv_b, qkv_b, gain.reshape(1, B_WIDTH))


def _layer_norm_rows(z, g, b):
    mu = jnp.mean(z, axis=-1, keepdims=True)
    zc = z - mu
    var = jnp.mean(zc * zc, axis=-1, keepdims=True)
    return zc * lax.rsqrt(var + LN_EPS) * g + b


def _out_kernel(ya_ref, yb_ref, ga_ref, w_ref, x_ref, g_ref, b_ref, wdn_ref,
                h_ref, wdn_bf_ref):
    wdn_bf_ref[...] = wdn_ref[...].astype(BF16)
    blocks = [slice(r, r + OUT_SUB_ROWS) for r in range(0, ya_ref.shape[0], OUT_SUB_ROWS)]
    mixes = []
    for rows in blocks:
        ya = ya_ref[rows, :].astype(F32)
        scale = _rms_scale(jnp.sum(ya * ya, axis=1, keepdims=True), A_WIDTH)
        ya = (ya * scale * ga_ref[...]).astype(BF16)
        mix = jnp.dot(ya, w_ref[0:A_WIDTH, :], preferred_element_type=F32)
        mixes.append(mix + jnp.dot(yb_ref[rows, :], w_ref[A_WIDTH:, :], preferred_element_type=F32))
    for rows, mix in zip(blocks, mixes):
        h_ref[rows, :] = _layer_norm_rows(ALPHA * x_ref[rows, :] + mix, g_ref[...], b_ref[...])


def _out_call(ya, yb, gain_a, w_out_bf, x2d, ln_g, ln_b, w_down):
    seq = x2d.shape[0]
    tm = OUT_ROWS
    steps = seq // tm
    down_rows = D_FF // steps
    row_spec = pl.BlockSpec((1, D_MODEL), lambda i: (0, 0))
    vmem = (D_MODEL * D_MODEL * 2 + 2 * 2 * tm * A_WIDTH * 2
            + 2 * 2 * tm * D_MODEL * 4 + 2 * tm * D_MODEL * 4
            + 2 * down_rows * D_MODEL * 6)
    return pl.pallas_call(
        _out_kernel,
        grid=(steps,),
        in_specs=[
            pl.BlockSpec((tm, A_WIDTH), lambda i: (i, 0)),
            pl.BlockSpec((tm, B_WIDTH), lambda i: (i, 0)),
            pl.BlockSpec((1, A_WIDTH), lambda i: (0, 0)),
            pl.BlockSpec((D_MODEL, D_MODEL), lambda i: (0, 0), pipeline_mode=pl.Buffered(1)),
            pl.BlockSpec((tm, D_MODEL), lambda i: (i, 0)),
            row_spec, row_spec,
            pl.BlockSpec((down_rows, D_MODEL), lambda i: (i, 0)),
        ],
        out_specs=[
            pl.BlockSpec((tm, D_MODEL), lambda i: (i, 0)),
            pl.BlockSpec((down_rows, D_MODEL), lambda i: (i, 0)),
        ],
        out_shape=[
            jax.ShapeDtypeStruct((seq, D_MODEL), F32),
            jax.ShapeDtypeStruct((D_FF, D_MODEL), BF16),
        ],
        compiler_params=_params(("arbitrary",), vmem),
        name="out_proj_ln",
    )(ya, yb, gain_a.reshape(1, A_WIDTH), w_out_bf, x2d,
      ln_g.reshape(1, D_MODEL), ln_b.reshape(1, D_MODEL), w_down)


def _ffn_kernel(h_ref, wg_ref, wu_ref, wd_ref, g_ref, b_ref, o_ref, hb_ref):
    j = pl.program_id(1)
    last = pl.num_programs(1) - 1
    half = wg_ref.shape[1] // 2
    halves = [slice(0, half), slice(half, 2 * half)]

    def activations(hb):
        gates = [jnp.dot(hb, wg_ref[:, s], preferred_element_type=F32) for s in halves]
        ups = [jnp.dot(hb, wu_ref[:, s], preferred_element_type=F32) for s in halves]
        return [(g * jax.nn.sigmoid(g) * u).astype(BF16) for g, u in zip(gates, ups)]

    def down(acts, rows=slice(None)):
        return (jnp.dot(acts[0][rows], wd_ref[halves[0], :], preferred_element_type=F32)
                + jnp.dot(acts[1][rows], wd_ref[halves[1], :], preferred_element_type=F32))

    @pl.when(j == 0)
    def _():
        h = h_ref[...]
        hb = h.astype(BF16)
        hb_ref[...] = hb
        o_ref[...] = ALPHA * h + down(activations(hb))

    @pl.when((j > 0) & (j < last))
    def _():
        o_ref[...] += down(activations(hb_ref[...]))

    @pl.when(j == last)
    def _():
        acts = activations(hb_ref[...])
        for r in range(0, o_ref.shape[0], FFN_NORM_ROWS):
            rows = slice(r, r + FFN_NORM_ROWS)
            o_ref[rows, :] = _layer_norm_rows(o_ref[rows, :] + down(acts, rows),
                                              g_ref[...], b_ref[...])


def _ffn_call(h, w_up_b, w_down_b, ln_g, ln_b):
    seq = h.shape[0]
    tm, tf = FFN_ROWS, FFN_CHUNK
    n_chunks = D_FF // tf
    row_spec = pl.BlockSpec((1, D_MODEL), lambda i, j: (0, 0))
    vmem = (2 * tm * D_MODEL * 4 + tm * D_MODEL * 2 + 2 * tm * D_MODEL * 4
            + 2 * 3 * D_MODEL * tf * 2 + 3 * tm * tf * 4)
    return pl.pallas_call(
        _ffn_kernel,
        grid=(seq // tm, n_chunks),
        in_specs=[
            pl.BlockSpec((tm, D_MODEL), lambda i, j: (i, 0)),
            pl.BlockSpec((D_MODEL, tf), lambda i, j: (0, j)),
            pl.BlockSpec((D_MODEL, tf), lambda i, j: (0, n_chunks + j)),
            pl.BlockSpec((tf, D_MODEL), lambda i, j: (j, 0)),
            row_spec, row_spec,
        ],
        out_specs=pl.BlockSpec((tm, D_MODEL), lambda i, j: (i, 0)),
        out_shape=jax.ShapeDtypeStruct((seq, D_MODEL), F32),
        scratch_shapes=[pltpu.VMEM((tm, D_MODEL), BF16)],
        compiler_params=_params(("arbitrary", "arbitrary"), vmem),
        name="swiglu_ln",
    )(h, w_up_b, w_up_b, w_down_b, ln_g.reshape(1, D_MODEL), ln_b.reshape(1, D_MODEL))


def _layer(x2d, w_in, b_in, sinks, g_mix_a, g_mix_b, w_out, ln1_g, ln1_b, w_up, w_down, ln2_g, ln2_b):
    qkv4, qkv16, qkv_b, w_out_b, w_up_b = _proj_call(x2d, w_in, b_in, w_out, w_up)
    ya = _attn_a_call(qkv4, qkv16)
    yb = _attn_b_call(qkv_b, sinks, g_mix_b)
    h, w_down_b = _out_call(ya, yb, g_mix_a, w_out_b, x2d, ln1_g, ln1_b, w_down)
    return _ffn_call(h, w_up_b, w_down_b, ln2_g, ln2_b)


def kernel(x, w_in, b_in, sinks, g_mix_a, g_mix_b, w_out, ln1_g, ln1_b, w_up, w_down, ln2_g, ln2_b):
    batch, seq, d_model = x.shape
    assert (batch, d_model) == (1, D_MODEL) and seq % A_TILE == 0
    h = x.reshape(seq, d_model)
    for i in range(w_in.shape[0]):
        h = _layer(h, w_in[i], b_in[i], sinks[i], g_mix_a[i], g_mix_b[i], w_out[i],
                   ln1_g[i], ln1_b[i], w_up[i], w_down[i], ln2_g[i], ln2_b[i])
    return h.reshape(batch, seq, d_model)
```

```python
import functools
import math

import jax
import jax.numpy as jnp
from jax import lax
from jax.experimental import pallas as pl
from jax.experimental.pallas import tpu as pltpu

F32 = jnp.float32
BF16 = jnp.bfloat16

D_MODEL = 2048
A_HEAD_DIM = 128
A_WIDTH = 1024
A_HEADS = 8
MID_DIL = 4
FAR_DIL = 16
B_HEAD_DIM = 64
B_WIDTH = 1024
B_HEADS = 16
B_KV_HEADS = 2
B_GROUP = 8
B_KV_WIDTH = 128
QKV_A_WIDTH = 3 * A_WIDTH
QKV_B_WIDTH = B_WIDTH + 2 * B_KV_WIDTH
IN_WIDTH = QKV_A_WIDTH + QKV_B_WIDTH
D_FF = 5632
BLOCK = 128
ROPE_THETA = 10000.0
ALPHA = 2.0 ** 0.25
LN_EPS = 1e-5
RMS_EPS = 1e-6
LOG2_E = math.log2(math.e)

LANES = 128
V7X_VMEM_BYTES = 64 * 1024 * 1024
V7X_VMEM_REQUEST_CAP = V7X_VMEM_BYTES - 2 * 1024 * 1024

PROJ_ROWS = 512
PROJ_CHUNK = 512
A_TILE = FAR_DIL * BLOCK
B_ROWS = 512
OUT_ROWS = 512
OUT_SUB_ROWS = 256
FFN_ROWS = 1024
FFN_CHUNK = 512
FFN_NORM_ROWS = 256

A_MID_ROWS = A_TILE // MID_DIL
NEAR_SLAB = BLOCK // MID_DIL
GROUP = 4
GROUPS_AHEAD = 1
B_ITEM_GROUP = 2


def _vmem_limit(estimate_bytes, result_bytes=0):
    request = estimate_bytes + estimate_bytes // 4
    if result_bytes:
        request = max(request, V7X_VMEM_BYTES - result_bytes + result_bytes // 8)
    return int(min(V7X_VMEM_REQUEST_CAP, request))


def _params(semantics, vmem_estimate, result_bytes=0):
    return pltpu.CompilerParams(dimension_semantics=semantics,
                                vmem_limit_bytes=_vmem_limit(vmem_estimate, result_bytes))


def _rope_factors(seq, tile, head_dim):
    half = head_dim // 2
    inv_freq = ROPE_THETA ** (-jnp.arange(half, dtype=F32) / half)
    reps = LANES // head_dim

    def expand(pos):
        ang = pos.astype(F32)[:, None] * inv_freq[None, :]
        cos = jnp.cos(ang)
        sin = jnp.sin(ang)
        return (jnp.tile(jnp.concatenate([cos, cos], axis=-1), (1, reps)),
                jnp.tile(jnp.concatenate([sin, sin], axis=-1), (1, reps)))

    return expand(jnp.arange(tile)) + expand(jnp.arange(seq // tile) * tile)


def _proj_kernel(x_ref, w_ref, b_ref, oca_ref, osa_ref, tca_ref, tsa_ref,
                 ocb_ref, osb_ref, tcb_ref, tsb_ref, wout_ref, wup_ref,
                 o4_ref, o16_ref, ob_ref, wout_bf_ref, wup_bf_ref, nat_buf, mid_buf):
    rows = x_ref.shape[0]
    step = pl.program_id(0)
    wout_bf_ref[...] = wout_ref[...].astype(BF16)
    wup_bf_ref[...] = wup_ref[...].astype(BF16)
    xb = x_ref[...].astype(BF16)
    lane = lax.broadcasted_iota(jnp.int32, (rows, LANES), 1)
    first_half_a = lane < (A_HEAD_DIM // 2)
    first_half_b = (lane % B_HEAD_DIM) < (B_HEAD_DIM // 2)

    def tables(off_c_ref, off_s_ref, start_c_ref, start_s_ref, first_half):
        oc, os_ = off_c_ref[...], off_s_ref[...]
        tc, ts = start_c_ref[pl.ds(step, 1), :], start_s_ref[pl.ds(step, 1), :]
        sin = ts * oc + tc * os_
        return tc * oc - ts * os_, jnp.where(first_half, -sin, sin)

    ca, sa = tables(oca_ref, osa_ref, tca_ref, tsa_ref, first_half_a)
    cb, sb = tables(ocb_ref, osb_ref, tcb_ref, tsb_ref, first_half_b)
    scale_a = LOG2_E / math.sqrt(A_HEAD_DIM)
    scale_b = LOG2_E / math.sqrt(B_HEAD_DIM)
    mid_rows = rows // MID_DIL
    far_rows = rows // FAR_DIL

    def rope_a(y):
        return y * ca + pltpu.roll(y, A_HEAD_DIM // 2, 1) * sa

    def rope_b(y):
        partner = jnp.where(first_half_b,
                            pltpu.roll(y, LANES - B_HEAD_DIM // 2, 1),
                            pltpu.roll(y, B_HEAD_DIM // 2, 1))
        return y * cb + partner * sb

    for c0 in range(0, IN_WIDTH, PROJ_CHUNK):
        c1 = min(c0 + PROJ_CHUNK, IN_WIDTH)
        acc = jnp.dot(xb, w_ref[:, c0:c1], preferred_element_type=F32) + b_ref[:, c0:c1]
        for g, col in enumerate(range(c0, c1, LANES)):
            y = acc[:, col - c0:col - c0 + LANES]
            if col < A_WIDTH:
                y = rope_a(y) * scale_a
            elif col < 2 * A_WIDTH:
                y = rope_a(y)
            elif col < QKV_A_WIDTH:
                pass
            elif col < QKV_A_WIDTH + B_WIDTH:
                y = rope_b(y) * scale_b
            elif col < QKV_A_WIDTH + B_WIDTH + B_KV_WIDTH:
                y = rope_b(y)
            if col >= QKV_A_WIDTH:
                ob_ref[:, col - QKV_A_WIDTH:col - QKV_A_WIDTH + LANES] = y.astype(BF16)
                continue
            nat_buf[g] = y
            for c in range(MID_DIL):
                cls = nat_buf[g, pl.ds(c, mid_rows, stride=MID_DIL), :]
                o4_ref[c, :, col:col + LANES] = cls.astype(BF16)
                mid_buf[g, c * mid_rows:(c + 1) * mid_rows, :] = cls
            for c in range(MID_DIL):
                for c2 in range(FAR_DIL // MID_DIL):
                    cls = mid_buf[g, pl.ds(c * mid_rows + c2, far_rows, stride=MID_DIL), :]
                    o16_ref[c + MID_DIL * c2, :, col:col + LANES] = cls.astype(BF16)


def _proj_call(x2d, w_in, b_in, w_out, w_up):
    seq = x2d.shape[0]
    tm = PROJ_ROWS
    steps = seq // tm
    rope_a = _rope_factors(seq, tm, A_HEAD_DIM)
    rope_b = _rope_factors(seq, tm, B_HEAD_DIM)
    offset_spec = pl.BlockSpec((tm, LANES), lambda i: (0, 0))
    start_spec = pl.BlockSpec((steps, LANES), lambda i: (0, 0))
    rope_specs = [offset_spec, offset_spec, start_spec, start_spec]
    wout_rows = D_MODEL // steps
    groups = PROJ_CHUNK // LANES
    vmem = (D_MODEL * IN_WIDTH * 2
            + 2 * tm * D_MODEL * 4
            + 2 * tm * (2 * QKV_A_WIDTH + QKV_B_WIDTH) * 2
            + 2 * 4 * (tm + steps) * LANES * 4
            + 2 * wout_rows * (D_MODEL + 2 * D_FF) * 6
            + 2 * groups * tm * LANES * 4
            + tm * D_MODEL * 2 + 2 * tm * PROJ_CHUNK * 4)
    return pl.pallas_call(
        _proj_kernel,
        grid=(steps,),
        in_specs=[
            pl.BlockSpec((tm, D_MODEL), lambda i: (i, 0)),
            pl.BlockSpec((D_MODEL, IN_WIDTH), lambda i: (0, 0), pipeline_mode=pl.Buffered(1)),
            pl.BlockSpec((1, IN_WIDTH), lambda i: (0, 0)),
            *rope_specs, *rope_specs,
            pl.BlockSpec((wout_rows, D_MODEL), lambda i: (i, 0)),
            pl.BlockSpec((wout_rows, 2 * D_FF), lambda i: (i, 0)),
        ],
        out_specs=[
            pl.BlockSpec((MID_DIL, tm // MID_DIL, QKV_A_WIDTH), lambda i: (0, i, 0)),
            pl.BlockSpec((FAR_DIL, tm // FAR_DIL, QKV_A_WIDTH), lambda i: (0, i, 0)),
            pl.BlockSpec((tm, QKV_B_WIDTH), lambda i: (i, 0)),
            pl.BlockSpec((wout_rows, D_MODEL), lambda i: (i, 0)),
            pl.BlockSpec((wout_rows, 2 * D_FF), lambda i: (i, 0)),
        ],
        out_shape=[
            jax.ShapeDtypeStruct((MID_DIL, seq // MID_DIL, QKV_A_WIDTH), BF16),
            jax.ShapeDtypeStruct((FAR_DIL, seq // FAR_DIL, QKV_A_WIDTH), BF16),
            jax.ShapeDtypeStruct((seq, QKV_B_WIDTH), BF16),
            jax.ShapeDtypeStruct((D_MODEL, D_MODEL), BF16),
            jax.ShapeDtypeStruct((D_MODEL, 2 * D_FF), BF16),
        ],
        scratch_shapes=[pltpu.VMEM((groups, tm, LANES), F32), pltpu.VMEM((groups, tm, LANES), F32)],
        compiler_params=_params(("arbitrary",), vmem),
        name="proj_rope",
    )(x2d, w_in.astype(BF16), b_in.reshape(1, IN_WIDTH), *rope_a, *rope_b, w_out, w_up)


def _band_masks(lo, is_first_tile):
    qi = lax.broadcasted_iota(jnp.int32, (BLOCK, 2 * BLOCK), 0)
    kj = lax.broadcasted_iota(jnp.int32, (BLOCK, 2 * BLOCK), 1)
    band = (kj >= qi + lo) & (kj <= qi + BLOCK)
    first_lower = jnp.where(is_first_tile, BLOCK, 0)
    return band, band & (kj >= first_lower)


def _near_masks(is_first_tile):
    iq = lax.broadcasted_iota(jnp.int32, (BLOCK, 2 * BLOCK), 0)
    ik = lax.broadcasted_iota(jnp.int32, (BLOCK, 2 * BLOCK), 1)
    cq, mq = iq // NEAR_SLAB, iq % NEAR_SLAB
    ck, mk = ik // (2 * NEAR_SLAB), ik % (2 * NEAR_SLAB)
    dist = MID_DIL * (mq + NEAR_SLAB - mk) + (cq - ck)
    valid = (dist >= 0) & (dist <= BLOCK)
    first_lower = jnp.where(is_first_tile, NEAR_SLAB, 0)
    return valid, valid & (mk >= first_lower)


def _rms_scale(sum_sq, width):
    return lax.rsqrt(sum_sq * (1.0 / width) + RMS_EPS)


def _qk(q, k):
    return lax.dot_general(q, k, (((1,), (1,)), ((), ())), preferred_element_type=F32)


def _attn_a_kernel(q4_ref, k4_ref, k4h_ref, v4_ref, v4h_ref,
                   q16_ref, k16_ref, k16h_ref, v16_ref, v16h_ref, y_ref,
                   k4buf, v4buf, k16buf, v16buf,
                   acc_f, m_f, l_f, acc_m, m_m, l_m, y_s):
    first = pl.program_id(0) == 0
    k4buf[:, 0:BLOCK, :] = k4h_ref[...]
    k4buf[:, BLOCK:, :] = k4_ref[...]
    v4buf[:, 0:BLOCK, :] = v4h_ref[...]
    v4buf[:, BLOCK:, :] = v4_ref[...]
    k16buf[:, 0:BLOCK, :] = k16h_ref[...]
    k16buf[:, BLOCK:, :] = k16_ref[...]
    v16buf[:, 0:BLOCK, :] = v16h_ref[...]
    v16buf[:, BLOCK:, :] = v16_ref[...]

    near, near_first = _near_masks(first)
    band, band_first = _band_masks(0, first)
    full = (BLOCK, LANES)
    ones = jnp.ones((2 * BLOCK, LANES), BF16)

    def far_item(c, c2):
        r = c + MID_DIL * c2
        rows = pl.ds(c * A_MID_ROWS + c2, BLOCK, stride=MID_DIL)

        def store(m, l, acc):
            acc_f[rows, :] = acc
            m_f[rows, :] = m
            l_f[rows, :] = l

        return (lambda: q16_ref[r], lambda: k16buf[r], lambda: v16buf[r], band_first, None, store)

    def mid_item(c, j):
        rows = slice(c * A_MID_ROWS + j * BLOCK, c * A_MID_ROWS + (j + 1) * BLOCK)

        def store(m, l, acc):
            acc_m[rows, :] = acc
            m_m[rows, :] = m
            l_m[rows, :] = l

        return (lambda: q4_ref[c, j * BLOCK:(j + 1) * BLOCK, :],
                lambda: k4buf[c, j * BLOCK:(j + 2) * BLOCK, :],
                lambda: v4buf[c, j * BLOCK:(j + 2) * BLOCK, :],
                band_first if j == 0 else band,
                lambda: (m_f[rows, :], l_f[rows, :], acc_f[rows, :]), store)

    def near_item(j):
        r0 = j * NEAR_SLAB
        h0 = BLOCK + r0 - NEAR_SLAB

        def gather(ref, start, size):
            return jnp.concatenate([ref[c, start:start + size, :] for c in range(MID_DIL)], axis=0)

        def state(buf):
            return jnp.concatenate([buf[c * A_MID_ROWS + r0:c * A_MID_ROWS + r0 + NEAR_SLAB, :]
                                    for c in range(MID_DIL)], axis=0)

        def store(m, l, acc):
            out = acc * (1.0 / l)
            for c in range(MID_DIL):
                y_s[pl.ds(j * BLOCK + c, NEAR_SLAB, stride=MID_DIL), :] = (
                    out[c * NEAR_SLAB:(c + 1) * NEAR_SLAB])

        return (lambda: gather(q4_ref, r0, NEAR_SLAB), lambda: gather(k4buf, h0, 2 * NEAR_SLAB),
                lambda: gather(v4buf, h0, 2 * NEAR_SLAB), near_first if j == 0 else near,
                lambda: (state(m_m), state(l_m), state(acc_m)), store)

    work = [far_item(c, c2) for c in range(MID_DIL) for c2 in range(FAR_DIL // MID_DIL)]
    work += [mid_item(c, j) for c in range(MID_DIL) for j in range(A_MID_ROWS // BLOCK)]
    work += [near_item(j) for j in range(A_TILE // BLOCK)]
    groups = [work[g:g + GROUP] for g in range(0, len(work), GROUP)]

    def score_group(items):
        return [_qk(q(), k()) for q, k, _, _, _, _ in items]

    def finish_group(items, scores):
        staged = []
        for (_, _, _, valid, old, _), s in zip(items, scores):
            s = jnp.where(valid, s, -jnp.inf)
            m = jnp.max(s, axis=1, keepdims=True)
            if old is None:
                staged.append((jnp.broadcast_to(m, full), jnp.exp2(s - m).astype(BF16), None))
                continue
            m_old, l_old, acc_old = old()
            m = jnp.maximum(m_old, m)
            p = jnp.concatenate([jnp.exp2(s[:, :LANES] - m), jnp.exp2(s[:, LANES:] - m)], axis=1)
            staged.append((m, p.astype(BF16), (jnp.exp2(m_old - m), l_old, acc_old)))
        for (_, _, v, _, _, store), (m, p, old) in zip(items, staged):
            pv = jnp.dot(p, jnp.concatenate([v(), ones], axis=1), preferred_element_type=F32)
            acc, l = pv[:, :LANES], pv[:, LANES:]
            if old is not None:
                rescale, l_old, acc_old = old
                acc = rescale * acc_old + acc
                l = rescale * l_old + l
            store(m, l, acc)

    pending = [score_group(items) for items in groups[:GROUPS_AHEAD]]
    for g, items in enumerate(groups):
        if g + GROUPS_AHEAD < len(groups):
            pending.append(score_group(groups[g + GROUPS_AHEAD]))
        finish_group(items, pending.pop(0))

    y_ref[...] = y_s[...].astype(BF16)


def _attn_a_call(qkv4, qkv16):
    seq = qkv4.shape[0] * qkv4.shape[1]
    tiles = seq // A_TILE
    mid_blocks = A_MID_ROWS // BLOCK

    def mid_tile(which):
        return pl.BlockSpec((MID_DIL, A_MID_ROWS, A_HEAD_DIM),
                            lambda t, h: (0, t, which * A_HEADS + h))

    def mid_halo(which):
        return pl.BlockSpec((MID_DIL, BLOCK, A_HEAD_DIM),
                            lambda t, h: (0, jnp.maximum(t * mid_blocks - 1, 0), which * A_HEADS + h))

    def far_tile(which):
        return pl.BlockSpec((FAR_DIL, BLOCK, A_HEAD_DIM), lambda t, h: (0, t, which * A_HEADS + h))

    def far_halo(which):
        return pl.BlockSpec((FAR_DIL, BLOCK, A_HEAD_DIM),
                            lambda t, h: (0, jnp.maximum(t - 1, 0), which * A_HEADS + h))

    slab = A_TILE * A_HEAD_DIM
    vmem = (2 * (3 * slab + 2 * slab // 4) * 2
            + 2 * 5 * slab * 2
            + 2 * slab * 2
            + 2 * (slab + slab // 4) * 2 + 2 * 2 * slab * 2
            + 7 * slab * 4)
    return pl.pallas_call(
        _attn_a_kernel,
        grid=(tiles, A_HEADS),
        in_specs=[mid_tile(0), mid_tile(1), mid_halo(1), mid_tile(2), mid_halo(2),
                  far_tile(0), far_tile(1), far_halo(1), far_tile(2), far_halo(2)],
        out_specs=pl.BlockSpec((A_TILE, A_HEAD_DIM), lambda t, h: (t, h)),
        out_shape=jax.ShapeDtypeStruct((seq, A_WIDTH), BF16),
        scratch_shapes=[
            pltpu.VMEM((MID_DIL, A_MID_ROWS + BLOCK, A_HEAD_DIM), BF16),
            pltpu.VMEM((MID_DIL, A_MID_ROWS + BLOCK, A_HEAD_DIM), BF16),
            pltpu.VMEM((FAR_DIL, 2 * BLOCK, A_HEAD_DIM), BF16),
            pltpu.VMEM((FAR_DIL, 2 * BLOCK, A_HEAD_DIM), BF16),
        ] + [pltpu.VMEM((A_TILE, LANES), F32)] * 7,
        compiler_params=_params(("arbitrary", "arbitrary"), vmem, result_bytes=seq * A_WIDTH * 2),
        name="attn_a",
    )(qkv4, qkv4, qkv4, qkv4, qkv4, qkv16, qkv16, qkv16, qkv16, qkv16)


def _attn_b_kernel(sinks_ref, q_ref, km_ref, kh_ref, vm_ref, vh_ref, gain_ref, wdn_ref,
                   y_ref, wdn_bf_ref, klo, khi, vlo, vhi, obuf, *, rows):
    wdn_bf_ref[...] = wdn_ref[...].astype(BF16)
    lane = lax.broadcasted_iota(jnp.int32, (rows + BLOCK, LANES), 1)
    low = lane < B_HEAD_DIM

    def fill(dst_lo, dst_hi, halo_ref, main_ref):
        full = jnp.concatenate([halo_ref[...], main_ref[...]], axis=0).astype(F32)
        swapped = pltpu.roll(full, B_HEAD_DIM, 1)
        zero = jnp.zeros_like(full)
        for c in range(B_KV_HEADS):
            own_low, own_high = (full, swapped) if c == 0 else (swapped, full)
            dst_lo[c] = jnp.where(low, own_low, zero).astype(BF16)
            dst_hi[c] = jnp.where(low, zero, own_high).astype(BF16)

    fill(klo, khi, kh_ref, km_ref)
    fill(vlo, vhi, vh_ref, vm_ref)

    band, band_first = _band_masks(1, pl.program_id(0) == 0)

    items = [(j, pair) for j in range(rows // BLOCK) for pair in range(B_HEADS // 2)]
    groups = [items[g:g + B_ITEM_GROUP] for g in range(0, len(items), B_ITEM_GROUP)]

    col0 = lax.broadcasted_iota(jnp.int32, (1, LANES), 1) == 0
    neg_inf_row = jnp.full((1, LANES), -jnp.inf, F32)

    def score_group(group):
        out = []
        for j, pair in group:
            c = (2 * pair) // B_GROUP
            r0 = j * BLOCK
            q = q_ref[r0:r0 + BLOCK, pair * LANES:(pair + 1) * LANES]
            valid = band_first if j == 0 else band
            pair_scores = []
            for odd, k_sel in enumerate((klo, khi)):
                s = _qk(q, k_sel[c, r0:r0 + 2 * BLOCK, :])
                sink_row = jnp.where(col0, sinks_ref[2 * pair + odd] * LOG2_E, neg_inf_row)
                pair_scores.append(jnp.concatenate(
                    [jnp.where(valid[:, :LANES], s[:, :LANES], sink_row),
                     jnp.where(valid[:, LANES:], s[:, LANES:], -jnp.inf)], axis=1))
            out.append(pair_scores)
        return out

    def finish_group(group, scores):
        staged = []
        for (j, pair), pair_scores in zip(group, scores):
            for odd, s in enumerate(pair_scores):
                m = jnp.max(s, axis=1, keepdims=True)
                p = jnp.exp2(s - m)
                l = jnp.sum(p, axis=1, keepdims=True)
                p = jnp.concatenate([jnp.where(col0, 0.0, p[:, :LANES]), p[:, LANES:]], axis=1)
                staged.append((p.astype(BF16), 1.0 / l))
        for i, (j, pair) in enumerate(group):
            c = (2 * pair) // B_GROUP
            r0 = j * BLOCK
            o = jnp.zeros((BLOCK, LANES), F32)
            for odd, v_sel in enumerate((vlo, vhi)):
                p, weight = staged[2 * i + odd]
                acc = jnp.dot(p, v_sel[c, r0:r0 + 2 * BLOCK, :], preferred_element_type=F32)
                o = o + acc * weight
            obuf[r0:r0 + BLOCK, pair * LANES:(pair + 1) * LANES] = o

    scores = score_group(groups[0])
    for g, group in enumerate(groups):
        upcoming = score_group(groups[g + 1]) if g + 1 < len(groups) else None
        finish_group(group, scores)
        scores = upcoming

    for j in range(rows // BLOCK):
        o = obuf[j * BLOCK:(j + 1) * BLOCK, :]
        scale = _rms_scale(jnp.sum(o * o, axis=1, keepdims=True), B_WIDTH)
        y_ref[j * BLOCK:(j + 1) * BLOCK, :] = (o * scale * gain_ref[...]).astype(BF16)


def _attn_b_call(qkv_b, sinks, gain, w_down):
    seq = qkv_b.shape[0]
    rows = B_ROWS
    blocks_per_tile = rows // BLOCK
    down_rows = D_FF // (seq // rows)
    k_col = B_WIDTH // B_KV_WIDTH
    v_col = k_col + 1

    def kv_tile(col):
        return pl.BlockSpec((rows, B_KV_WIDTH), lambda t: (t, col))

    def kv_halo(col):
        return pl.BlockSpec((BLOCK, B_KV_WIDTH),
                            lambda t: (jnp.maximum(t * blocks_per_tile - 1, 0), col))

    kv_scratch = pltpu.VMEM((B_KV_HEADS, rows + BLOCK, LANES), BF16)
    vmem = (2 * 2 * rows * B_WIDTH * 2 + 4 * 2 * (rows + BLOCK) * LANES * 2
            + 4 * B_KV_HEADS * (rows + BLOCK) * LANES * 2 + rows * B_WIDTH * 4
            + 4 * (rows + BLOCK) * LANES * 4
            + 2 * down_rows * D_MODEL * 6)
    return pl.pallas_call(
        functools.partial(_attn_b_kernel, rows=rows),
        grid=(seq // rows,),
        in_specs=[
            pl.BlockSpec(memory_space=pltpu.SMEM),
            pl.BlockSpec((rows, B_WIDTH), lambda t: (t, 0)),
            kv_tile(k_col), kv_halo(k_col), kv_tile(v_col), kv_halo(v_col),
            pl.BlockSpec((1, B_WIDTH), lambda t: (0, 0)),
            pl.BlockSpec((down_rows, D_MODEL), lambda t: (t, 0)),
        ],
        out_specs=[
            pl.BlockSpec((rows, B_WIDTH), lambda t: (t, 0)),
            pl.BlockSpec((down_rows, D_MODEL), lambda t: (t, 0)),
        ],
        out_shape=[
            jax.ShapeDtypeStruct((seq, B_WIDTH), BF16),
            jax.ShapeDtypeStruct((D_FF, D_MODEL), BF16),
        ],
        scratch_shapes=[kv_scratch, kv_scratch, kv_scratch, kv_scratch,
                        pltpu.VMEM((rows, B_WIDTH), F32)],
        compiler_params=_params(("arbitrary",), vmem, result_bytes=seq * B_WIDTH * 2),
        name="attn_b",
    )(sinks, qkv_b, qkv_b, qkv_b, qkv_b, qkv_b, gain.reshape(1, B_WIDTH), w_down)


def _layer_norm_rows(z, g, b):
    mu = jnp.mean(z, axis=-1, keepdims=True)
    zc = z - mu
    var = jnp.mean(zc * zc, axis=-1, keepdims=True)
    return zc * lax.rsqrt(var + LN_EPS) * g + b


def _out_kernel(ya_ref, yb_ref, ga_ref, w_ref, x_ref, g_ref, b_ref, h_ref):
    blocks = [slice(r, r + OUT_SUB_ROWS) for r in range(0, ya_ref.shape[0], OUT_SUB_ROWS)]
    mixes = []
    for rows in blocks:
        ya = ya_ref[rows, :].astype(F32)
        scale = _rms_scale(jnp.sum(ya * ya, axis=1, keepdims=True), A_WIDTH)
        ya = (ya * scale * ga_ref[...]).astype(BF16)
        mix = jnp.dot(ya, w_ref[0:A_WIDTH, :], preferred_element_type=F32)
        mixes.append(mix + jnp.dot(yb_ref[rows, :], w_ref[A_WIDTH:, :], preferred_element_type=F32))
    for rows, mix in zip(blocks, mixes):
        h_ref[rows, :] = _layer_norm_rows(ALPHA * x_ref[rows, :] + mix, g_ref[...], b_ref[...])


def _out_call(ya, yb, gain_a, w_out_bf, x2d, ln_g, ln_b):
    seq = x2d.shape[0]
    tm = OUT_ROWS
    row_spec = pl.BlockSpec((1, D_MODEL), lambda i: (0, 0))
    vmem = (D_MODEL * D_MODEL * 2 + 2 * 2 * tm * A_WIDTH * 2
            + 2 * 2 * tm * D_MODEL * 4 + 2 * tm * D_MODEL * 4)
    return pl.pallas_call(
        _out_kernel,
        grid=(seq // tm,),
        in_specs=[
            pl.BlockSpec((tm, A_WIDTH), lambda i: (i, 0)),
            pl.BlockSpec((tm, B_WIDTH), lambda i: (i, 0)),
            pl.BlockSpec((1, A_WIDTH), lambda i: (0, 0)),
            pl.BlockSpec((D_MODEL, D_MODEL), lambda i: (0, 0), pipeline_mode=pl.Buffered(1)),
            pl.BlockSpec((tm, D_MODEL), lambda i: (i, 0)),
            row_spec, row_spec,
        ],
        out_specs=pl.BlockSpec((tm, D_MODEL), lambda i: (i, 0)),
        out_shape=jax.ShapeDtypeStruct((seq, D_MODEL), F32),
        compiler_params=_params(("arbitrary",), vmem),
        name="out_proj_ln",
    )(ya, yb, gain_a.reshape(1, A_WIDTH), w_out_bf, x2d,
      ln_g.reshape(1, D_MODEL), ln_b.reshape(1, D_MODEL))


def _ffn_kernel(h_ref, wg_ref, wu_ref, wd_ref, g_ref, b_ref, o_ref, hb_ref):
    j = pl.program_id(1)
    last = pl.num_programs(1) - 1
    half = wg_ref.shape[1] // 2
    halves = [slice(0, half), slice(half, 2 * half)]

    def activations(hb):
        gates = [jnp.dot(hb, wg_ref[:, s], preferred_element_type=F32) for s in halves]
        ups = [jnp.dot(hb, wu_ref[:, s], preferred_element_type=F32) for s in halves]
        return [(g * jax.nn.sigmoid(g) * u).astype(BF16) for g, u in zip(gates, ups)]

    def down(acts, rows=slice(None)):
        return (jnp.dot(acts[0][rows], wd_ref[halves[0], :], preferred_element_type=F32)
                + jnp.dot(acts[1][rows], wd_ref[halves[1], :], preferred_element_type=F32))

    @pl.when(j == 0)
    def _():
        h = h_ref[...]
        hb = h.astype(BF16)
        hb_ref[...] = hb
        o_ref[...] = ALPHA * h + down(activations(hb))

    @pl.when((j > 0) & (j < last))
    def _():
        o_ref[...] += down(activations(hb_ref[...]))

    @pl.when(j == last)
    def _():
        acts = activations(hb_ref[...])
        for r in range(0, o_ref.shape[0], FFN_NORM_ROWS):
            rows = slice(r, r + FFN_NORM_ROWS)
            o_ref[rows, :] = _layer_norm_rows(o_ref[rows, :] + down(acts, rows),
                                              g_ref[...], b_ref[...])


def _ffn_call(h, w_up_b, w_down_b, ln_g, ln_b):
    seq = h.shape[0]
    tm, tf = FFN_ROWS, FFN_CHUNK
    n_chunks = D_FF // tf
    row_spec = pl.BlockSpec((1, D_MODEL), lambda i, j: (0, 0))
    vmem = (2 * tm * D_MODEL * 4 + tm * D_MODEL * 2 + 2 * tm * D_MODEL * 4
            + 2 * 3 * D_MODEL * tf * 2 + 3 * tm * tf * 4)
    return pl.pallas_call(
        _ffn_kernel,
        grid=(seq // tm, n_chunks),
        in_specs=[
            pl.BlockSpec((tm, D_MODEL), lambda i, j: (i, 0)),
            pl.BlockSpec((D_MODEL, tf), lambda i, j: (0, j)),
            pl.BlockSpec((D_MODEL, tf), lambda i, j: (0, n_chunks + j)),
            pl.BlockSpec((tf, D_MODEL), lambda i, j: (j, 0)),
            row_spec, row_spec,
        ],
        out_specs=pl.BlockSpec((tm, D_MODEL), lambda i, j: (i, 0)),
        out_shape=jax.ShapeDtypeStruct((seq, D_MODEL), F32),
        scratch_shapes=[pltpu.VMEM((tm, D_MODEL), BF16)],
        compiler_params=_params(("arbitrary", "arbitrary"), vmem),
        name="swiglu_ln",
    )(h, w_up_b, w_up_b, w_down_b, ln_g.reshape(1, D_MODEL), ln_b.reshape(1, D_MODEL))


def _layer(x2d, w_in, b_in, sinks, g_mix_a, g_mix_b, w_out, ln1_g, ln1_b, w_up, w_down, ln2_g, ln2_b):
    qkv4, qkv16, qkv_b, w_out_b, w_up_b = _proj_call(x2d, w_in, b_in, w_out, w_up)
    ya = _attn_a_call(qkv4, qkv16)
    yb, w_down_b = _attn_b_call(qkv_b, sinks, g_mix_b, w_down)
    h = _out_call(ya, yb, g_mix_a, w_out_b, x2d, ln1_g, ln1_b)
    return _ffn_call(h, w_up_b, w_down_b, ln2_g, ln2_b)


def kernel(x, w_in, b_in, sinks, g_mix_a, g_mix_b, w_out, ln1_g, ln1_b, w_up, w_down, ln2_g, ln2_b):
    batch, seq, d_model = x.shape
    assert (batch, d_model) == (1, D_MODEL) and seq % A_TILE == 0
    h = x.reshape(seq, d_model)
    for i in range(w_in.shape[0]):
        h = _layer(h, w_in[i], b_in[i], sinks[i], g_mix_a[i], g_mix_b[i], w_out[i],
                   ln1_g[i], ln1_b[i], w_up[i], w_down[i], ln2_g[i], ln2_b[i])
    return h.reshape(batch, seq, d_model)
```

```python
import functools
import math

import jax
import jax.numpy as jnp
from jax import lax
from jax.experimental import pallas as pl
from jax.experimental.pallas import tpu as pltpu

F32 = jnp.float32
BF16 = jnp.bfloat16

D_MODEL = 2048
A_HEAD_DIM = 128
A_WIDTH = 1024
A_HEADS = 8
MID_DIL = 4
FAR_DIL = 16
B_HEAD_DIM = 64
B_WIDTH = 1024
B_HEADS = 16
B_KV_HEADS = 2
B_GROUP = 8
B_KV_WIDTH = 128
QKV_A_WIDTH = 3 * A_WIDTH
QKV_B_WIDTH = B_WIDTH + 2 * B_KV_WIDTH
IN_WIDTH = QKV_A_WIDTH + QKV_B_WIDTH
D_FF = 5632
BLOCK = 128
ROPE_THETA = 10000.0
ALPHA = 2.0 ** 0.25
LN_EPS = 1e-5
RMS_EPS = 1e-6
LOG2_E = math.log2(math.e)

LANES = 128
V7X_VMEM_BYTES = 64 * 1024 * 1024
V7X_VMEM_REQUEST_CAP = V7X_VMEM_BYTES - 2 * 1024 * 1024

PROJ_ROWS = 512
PROJ_CHUNK = 512
A_TILE = FAR_DIL * BLOCK
B_ROWS = 512
OUT_ROWS = 512
OUT_SUB_ROWS = 256
FFN_ROWS = 1024
FFN_CHUNK = 512
FFN_NORM_ROWS = 256

A_MID_ROWS = A_TILE // MID_DIL
NEAR_SLAB = BLOCK // MID_DIL
GROUP = 4
GROUPS_AHEAD = 1
B_ITEM_GROUP = 2


def _vmem_limit(estimate_bytes, result_bytes=0):
    request = estimate_bytes + estimate_bytes // 4
    if result_bytes:
        request = max(request, V7X_VMEM_BYTES - result_bytes + result_bytes // 8)
    return int(min(V7X_VMEM_REQUEST_CAP, request))


def _params(semantics, vmem_estimate, result_bytes=0):
    return pltpu.CompilerParams(dimension_semantics=semantics,
                                vmem_limit_bytes=_vmem_limit(vmem_estimate, result_bytes))


def _rope_factors(seq, tile, head_dim):
    half = head_dim // 2
    inv_freq = ROPE_THETA ** (-jnp.arange(half, dtype=F32) / half)
    reps = LANES // head_dim

    def expand(pos):
        ang = pos.astype(F32)[:, None] * inv_freq[None, :]
        cos = jnp.cos(ang)
        sin = jnp.sin(ang)
        return (jnp.tile(jnp.concatenate([cos, cos], axis=-1), (1, reps)),
                jnp.tile(jnp.concatenate([sin, sin], axis=-1), (1, reps)))

    return expand(jnp.arange(tile)) + expand(jnp.arange(seq // tile) * tile)


def _proj_kernel(x_ref, w_ref, b_ref, oca_ref, osa_ref, tca_ref, tsa_ref,
                 ocb_ref, osb_ref, tcb_ref, tsb_ref, wout_ref, wup_ref,
                 o4_ref, o16_ref, ob_ref, wout_bf_ref, wup_bf_ref, nat_buf, mid_buf):
    rows = x_ref.shape[0]
    step = pl.program_id(0)
    wout_bf_ref[...] = wout_ref[...].astype(BF16)
    wup_bf_ref[...] = wup_ref[...].astype(BF16)
    xb = x_ref[...].astype(BF16)
    lane = lax.broadcasted_iota(jnp.int32, (rows, LANES), 1)
    first_half_a = lane < (A_HEAD_DIM // 2)
    first_half_b = (lane % B_HEAD_DIM) < (B_HEAD_DIM // 2)

    def tables(off_c_ref, off_s_ref, start_c_ref, start_s_ref, first_half):
        oc, os_ = off_c_ref[...], off_s_ref[...]
        tc, ts = start_c_ref[pl.ds(step, 1), :], start_s_ref[pl.ds(step, 1), :]
        sin = ts * oc + tc * os_
        return tc * oc - ts * os_, jnp.where(first_half, -sin, sin)

    ca, sa = tables(oca_ref, osa_ref, tca_ref, tsa_ref, first_half_a)
    cb, sb = tables(ocb_ref, osb_ref, tcb_ref, tsb_ref, first_half_b)
    scale_a = LOG2_E / math.sqrt(A_HEAD_DIM)
    scale_b = LOG2_E / math.sqrt(B_HEAD_DIM)
    mid_rows = rows // MID_DIL
    far_rows = rows // FAR_DIL

    def rope_a(y):
        return y * ca + pltpu.roll(y, A_HEAD_DIM // 2, 1) * sa

    def rope_b(y):
        partner = jnp.where(first_half_b,
                            pltpu.roll(y, LANES - B_HEAD_DIM // 2, 1),
                            pltpu.roll(y, B_HEAD_DIM // 2, 1))
        return y * cb + partner * sb

    starts = list(range(0, IN_WIDTH, PROJ_CHUNK))
    b_query = [c for c in starts if QKV_A_WIDTH <= c < QKV_A_WIDTH + B_WIDTH]
    for c0 in b_query + [c for c in starts if c not in b_query]:
        c1 = min(c0 + PROJ_CHUNK, IN_WIDTH)
        acc = jnp.dot(xb, w_ref[:, c0:c1], preferred_element_type=F32) + b_ref[:, c0:c1]
        for g, col in enumerate(range(c0, c1, LANES)):
            y = acc[:, col - c0:col - c0 + LANES]
            if col < A_WIDTH:
                y = rope_a(y) * scale_a
            elif col < 2 * A_WIDTH:
                y = rope_a(y)
            elif col < QKV_A_WIDTH:
                pass
            elif col < QKV_A_WIDTH + B_WIDTH:
                y = rope_b(y) * scale_b
            elif col < QKV_A_WIDTH + B_WIDTH + B_KV_WIDTH:
                y = rope_b(y)
            if col >= QKV_A_WIDTH:
                ob_ref[:, col - QKV_A_WIDTH:col - QKV_A_WIDTH + LANES] = y.astype(BF16)
                continue
            nat_buf[g] = y
            for c in range(MID_DIL):
                cls = nat_buf[g, pl.ds(c, mid_rows, stride=MID_DIL), :]
                o4_ref[c, :, col:col + LANES] = cls.astype(BF16)
                mid_buf[g, c * mid_rows:(c + 1) * mid_rows, :] = cls
            for c in range(MID_DIL):
                for c2 in range(FAR_DIL // MID_DIL):
                    cls = mid_buf[g, pl.ds(c * mid_rows + c2, far_rows, stride=MID_DIL), :]
                    o16_ref[c + MID_DIL * c2, :, col:col + LANES] = cls.astype(BF16)


def _proj_call(x2d, w_in, b_in, w_out, w_up):
    seq = x2d.shape[0]
    tm = PROJ_ROWS
    steps = seq // tm
    rope_a = _rope_factors(seq, tm, A_HEAD_DIM)
    rope_b = _rope_factors(seq, tm, B_HEAD_DIM)
    offset_spec = pl.BlockSpec((tm, LANES), lambda i: (0, 0))
    start_spec = pl.BlockSpec((steps, LANES), lambda i: (0, 0))
    rope_specs = [offset_spec, offset_spec, start_spec, start_spec]
    wout_rows = D_MODEL // steps
    groups = PROJ_CHUNK // LANES
    vmem = (D_MODEL * IN_WIDTH * 2
            + 2 * tm * D_MODEL * 4
            + 2 * tm * (2 * QKV_A_WIDTH + QKV_B_WIDTH) * 2
            + 2 * 4 * (tm + steps) * LANES * 4
            + 2 * wout_rows * (D_MODEL + 2 * D_FF) * 6
            + 2 * groups * tm * LANES * 4
            + tm * D_MODEL * 2 + 2 * tm * PROJ_CHUNK * 4)
    return pl.pallas_call(
        _proj_kernel,
        grid=(steps,),
        in_specs=[
            pl.BlockSpec((tm, D_MODEL), lambda i: (i, 0)),
            pl.BlockSpec((D_MODEL, IN_WIDTH), lambda i: (0, 0), pipeline_mode=pl.Buffered(1)),
            pl.BlockSpec((1, IN_WIDTH), lambda i: (0, 0)),
            *rope_specs, *rope_specs,
            pl.BlockSpec((wout_rows, D_MODEL), lambda i: (i, 0)),
            pl.BlockSpec((wout_rows, 2 * D_FF), lambda i: (i, 0)),
        ],
        out_specs=[
            pl.BlockSpec((MID_DIL, tm // MID_DIL, QKV_A_WIDTH), lambda i: (0, i, 0)),
            pl.BlockSpec((FAR_DIL, tm // FAR_DIL, QKV_A_WIDTH), lambda i: (0, i, 0)),
            pl.BlockSpec((tm, QKV_B_WIDTH), lambda i: (i, 0)),
            pl.BlockSpec((wout_rows, D_MODEL), lambda i: (i, 0)),
            pl.BlockSpec((wout_rows, 2 * D_FF), lambda i: (i, 0)),
        ],
        out_shape=[
            jax.ShapeDtypeStruct((MID_DIL, seq // MID_DIL, QKV_A_WIDTH), BF16),
            jax.ShapeDtypeStruct((FAR_DIL, seq // FAR_DIL, QKV_A_WIDTH), BF16),
            jax.ShapeDtypeStruct((seq, QKV_B_WIDTH), BF16),
            jax.ShapeDtypeStruct((D_MODEL, D_MODEL), BF16),
            jax.ShapeDtypeStruct((D_MODEL, 2 * D_FF), BF16),
        ],
        scratch_shapes=[pltpu.VMEM((groups, tm, LANES), F32), pltpu.VMEM((groups, tm, LANES), F32)],
        compiler_params=_params(("arbitrary",), vmem),
        name="proj_rope",
    )(x2d, w_in.astype(BF16), b_in.reshape(1, IN_WIDTH), *rope_a, *rope_b, w_out, w_up)


def _band_masks(lo, is_first_tile):
    qi = lax.broadcasted_iota(jnp.int32, (BLOCK, 2 * BLOCK), 0)
    kj = lax.broadcasted_iota(jnp.int32, (BLOCK, 2 * BLOCK), 1)
    band = (kj >= qi + lo) & (kj <= qi + BLOCK)
    first_lower = jnp.where(is_first_tile, BLOCK, 0)
    return band, band & (kj >= first_lower)


def _near_masks(is_first_tile):
    iq = lax.broadcasted_iota(jnp.int32, (BLOCK, 2 * BLOCK), 0)
    ik = lax.broadcasted_iota(jnp.int32, (BLOCK, 2 * BLOCK), 1)
    cq, mq = iq // NEAR_SLAB, iq % NEAR_SLAB
    ck, mk = ik // (2 * NEAR_SLAB), ik % (2 * NEAR_SLAB)
    dist = MID_DIL * (mq + NEAR_SLAB - mk) + (cq - ck)
    valid = (dist >= 0) & (dist <= BLOCK)
    first_lower = jnp.where(is_first_tile, NEAR_SLAB, 0)
    return valid, valid & (mk >= first_lower)


def _rms_scale(sum_sq, width):
    return lax.rsqrt(sum_sq * (1.0 / width) + RMS_EPS)


def _qk(q, k):
    return lax.dot_general(q, k, (((1,), (1,)), ((), ())), preferred_element_type=F32)


def _attn_a_kernel(q4_ref, k4_ref, k4h_ref, v4_ref, v4h_ref,
                   q16_ref, k16_ref, k16h_ref, v16_ref, v16h_ref, y_ref,
                   k4buf, v4buf, k16buf, v16buf,
                   acc_f, m_f, l_f, acc_m, m_m, l_m, y_s):
    first = pl.program_id(0) == 0
    k4buf[:, 0:BLOCK, :] = k4h_ref[...]
    k4buf[:, BLOCK:, :] = k4_ref[...]
    v4buf[:, 0:BLOCK, :] = v4h_ref[...]
    v4buf[:, BLOCK:, :] = v4_ref[...]
    k16buf[:, 0:BLOCK, :] = k16h_ref[...]
    k16buf[:, BLOCK:, :] = k16_ref[...]
    v16buf[:, 0:BLOCK, :] = v16h_ref[...]
    v16buf[:, BLOCK:, :] = v16_ref[...]

    near, near_first = _near_masks(first)
    band, band_first = _band_masks(0, first)
    full = (BLOCK, LANES)
    ones = jnp.ones((2 * BLOCK, LANES), BF16)

    def far_item(c, c2):
        r = c + MID_DIL * c2
        rows = pl.ds(c * A_MID_ROWS + c2, BLOCK, stride=MID_DIL)

        def store(m, l, acc):
            acc_f[rows, :] = acc
            m_f[rows, :] = m
            l_f[rows, :] = l

        return (lambda: q16_ref[r], lambda: k16buf[r], lambda: v16buf[r], band_first, None, store)

    def mid_item(c, j):
        rows = slice(c * A_MID_ROWS + j * BLOCK, c * A_MID_ROWS + (j + 1) * BLOCK)

        def store(m, l, acc):
            acc_m[rows, :] = acc
            m_m[rows, :] = m
            l_m[rows, :] = l

        return (lambda: q4_ref[c, j * BLOCK:(j + 1) * BLOCK, :],
                lambda: k4buf[c, j * BLOCK:(j + 2) * BLOCK, :],
                lambda: v4buf[c, j * BLOCK:(j + 2) * BLOCK, :],
                band_first if j == 0 else band,
                lambda: (m_f[rows, :], l_f[rows, :], acc_f[rows, :]), store)

    def near_item(j):
        r0 = j * NEAR_SLAB
        h0 = BLOCK + r0 - NEAR_SLAB

        def gather(ref, start, size):
            return jnp.concatenate([ref[c, start:start + size, :] for c in range(MID_DIL)], axis=0)

        def state(buf):
            return jnp.concatenate([buf[c * A_MID_ROWS + r0:c * A_MID_ROWS + r0 + NEAR_SLAB, :]
                                    for c in range(MID_DIL)], axis=0)

        def store(m, l, acc):
            out = acc * (1.0 / l)
            for c in range(MID_DIL):
                y_s[pl.ds(j * BLOCK + c, NEAR_SLAB, stride=MID_DIL), :] = (
                    out[c * NEAR_SLAB:(c + 1) * NEAR_SLAB])

        return (lambda: gather(q4_ref, r0, NEAR_SLAB), lambda: gather(k4buf, h0, 2 * NEAR_SLAB),
                lambda: gather(v4buf, h0, 2 * NEAR_SLAB), near_first if j == 0 else near,
                lambda: (state(m_m), state(l_m), state(acc_m)), store)

    work = [far_item(c, c2) for c in range(MID_DIL) for c2 in range(FAR_DIL // MID_DIL)]
    work += [mid_item(c, j) for c in range(MID_DIL) for j in range(A_MID_ROWS // BLOCK)]
    work += [near_item(j) for j in range(A_TILE // BLOCK)]
    groups = [work[g:g + GROUP] for g in range(0, len(work), GROUP)]

    def score_group(items):
        return [_qk(q(), k()) for q, k, _, _, _, _ in items]

    def finish_group(items, scores):
        staged = []
        for (_, _, _, valid, old, _), s in zip(items, scores):
            s = jnp.where(valid, s, -jnp.inf)
            m = jnp.max(s, axis=1, keepdims=True)
            if old is None:
                staged.append((jnp.broadcast_to(m, full), jnp.exp2(s - m).astype(BF16), None))
                continue
            m_old, l_old, acc_old = old()
            m = jnp.maximum(m_old, m)
            p = jnp.concatenate([jnp.exp2(s[:, :LANES] - m), jnp.exp2(s[:, LANES:] - m)], axis=1)
            staged.append((m, p.astype(BF16), (jnp.exp2(m_old - m), l_old, acc_old)))
        for (_, _, v, _, _, store), (m, p, old) in zip(items, staged):
            pv = jnp.dot(p, jnp.concatenate([v(), ones], axis=1), preferred_element_type=F32)
            acc, l = pv[:, :LANES], pv[:, LANES:]
            if old is not None:
                rescale, l_old, acc_old = old
                acc = rescale * acc_old + acc
                l = rescale * l_old + l
            store(m, l, acc)

    pending = [score_group(items) for items in groups[:GROUPS_AHEAD]]
    for g, items in enumerate(groups):
        if g + GROUPS_AHEAD < len(groups):
            pending.append(score_group(groups[g + GROUPS_AHEAD]))
        finish_group(items, pending.pop(0))

    y_ref[...] = y_s[...].astype(BF16)


def _attn_a_call(qkv4, qkv16):
    seq = qkv4.shape[0] * qkv4.shape[1]
    tiles = seq // A_TILE
    mid_blocks = A_MID_ROWS // BLOCK

    def mid_tile(which):
        return pl.BlockSpec((MID_DIL, A_MID_ROWS, A_HEAD_DIM),
                            lambda t, h: (0, t, which * A_HEADS + h))

    def mid_halo(which):
        return pl.BlockSpec((MID_DIL, BLOCK, A_HEAD_DIM),
                            lambda t, h: (0, jnp.maximum(t * mid_blocks - 1, 0), which * A_HEADS + h))

    def far_tile(which):
        return pl.BlockSpec((FAR_DIL, BLOCK, A_HEAD_DIM), lambda t, h: (0, t, which * A_HEADS + h))

    def far_halo(which):
        return pl.BlockSpec((FAR_DIL, BLOCK, A_HEAD_DIM),
                            lambda t, h: (0, jnp.maximum(t - 1, 0), which * A_HEADS + h))

    slab = A_TILE * A_HEAD_DIM
    vmem = (2 * (3 * slab + 2 * slab // 4) * 2
            + 2 * 5 * slab * 2
            + 2 * slab * 2
            + 2 * (slab + slab // 4) * 2 + 2 * 2 * slab * 2
            + 7 * slab * 4)
    return pl.pallas_call(
        _attn_a_kernel,
        grid=(tiles, A_HEADS),
        in_specs=[mid_tile(0), mid_tile(1), mid_halo(1), mid_tile(2), mid_halo(2),
                  far_tile(0), far_tile(1), far_halo(1), far_tile(2), far_halo(2)],
        out_specs=pl.BlockSpec((A_TILE, A_HEAD_DIM), lambda t, h: (t, h)),
        out_shape=jax.ShapeDtypeStruct((seq, A_WIDTH), BF16),
        scratch_shapes=[
            pltpu.VMEM((MID_DIL, A_MID_ROWS + BLOCK, A_HEAD_DIM), BF16),
            pltpu.VMEM((MID_DIL, A_MID_ROWS + BLOCK, A_HEAD_DIM), BF16),
            pltpu.VMEM((FAR_DIL, 2 * BLOCK, A_HEAD_DIM), BF16),
            pltpu.VMEM((FAR_DIL, 2 * BLOCK, A_HEAD_DIM), BF16),
        ] + [pltpu.VMEM((A_TILE, LANES), F32)] * 7,
        compiler_params=_params(("arbitrary", "arbitrary"), vmem, result_bytes=seq * A_WIDTH * 2),
        name="attn_a",
    )(qkv4, qkv4, qkv4, qkv4, qkv4, qkv16, qkv16, qkv16, qkv16, qkv16)


def _attn_b_kernel(sinks_ref, q_ref, km_ref, kh_ref, vm_ref, vh_ref, gain_ref, wdn_ref,
                   y_ref, wdn_bf_ref, klo, khi, vlo, vhi, obuf, *, rows):
    wdn_bf_ref[...] = wdn_ref[...].astype(BF16)
    lane = lax.broadcasted_iota(jnp.int32, (rows + BLOCK, LANES), 1)
    low = lane < B_HEAD_DIM

    def fill(dst_lo, dst_hi, halo_ref, main_ref):
        full = jnp.concatenate([halo_ref[...], main_ref[...]], axis=0).astype(F32)
        swapped = pltpu.roll(full, B_HEAD_DIM, 1)
        zero = jnp.zeros_like(full)
        for c in range(B_KV_HEADS):
            own_low, own_high = (full, swapped) if c == 0 else (swapped, full)
            dst_lo[c] = jnp.where(low, own_low, zero).astype(BF16)
            dst_hi[c] = jnp.where(low, zero, own_high).astype(BF16)

    fill(klo, khi, kh_ref, km_ref)
    fill(vlo, vhi, vh_ref, vm_ref)

    band, band_first = _band_masks(1, pl.program_id(0) == 0)

    items = [(j, pair) for j in range(rows // BLOCK) for pair in range(B_HEADS // 2)]
    groups = [items[g:g + B_ITEM_GROUP] for g in range(0, len(items), B_ITEM_GROUP)]

    col0 = lax.broadcasted_iota(jnp.int32, (1, LANES), 1) == 0
    neg_inf_row = jnp.full((1, LANES), -jnp.inf, F32)

    def score_group(group):
        out = []
        for j, pair in group:
            c = (2 * pair) // B_GROUP
            r0 = j * BLOCK
            q = q_ref[r0:r0 + BLOCK, pair * LANES:(pair + 1) * LANES]
            valid = band_first if j == 0 else band
            pair_scores = []
            for odd, k_sel in enumerate((klo, khi)):
                s = _qk(q, k_sel[c, r0:r0 + 2 * BLOCK, :])
                sink_row = jnp.where(col0, sinks_ref[2 * pair + odd] * LOG2_E, neg_inf_row)
                pair_scores.append(jnp.concatenate(
                    [jnp.where(valid[:, :LANES], s[:, :LANES], sink_row),
                     jnp.where(valid[:, LANES:], s[:, LANES:], -jnp.inf)], axis=1))
            out.append(pair_scores)
        return out

    def finish_group(group, scores):
        staged = []
        for (j, pair), pair_scores in zip(group, scores):
            for odd, s in enumerate(pair_scores):
                m = jnp.max(s, axis=1, keepdims=True)
                p = jnp.exp2(s - m)
                l = jnp.sum(p, axis=1, keepdims=True)
                p = jnp.concatenate([jnp.where(col0, 0.0, p[:, :LANES]), p[:, LANES:]], axis=1)
                staged.append((p.astype(BF16), 1.0 / l))
        for i, (j, pair) in enumerate(group):
            c = (2 * pair) // B_GROUP
            r0 = j * BLOCK
            o = jnp.zeros((BLOCK, LANES), F32)
            for odd, v_sel in enumerate((vlo, vhi)):
                p, weight = staged[2 * i + odd]
                acc = jnp.dot(p, v_sel[c, r0:r0 + 2 * BLOCK, :], preferred_element_type=F32)
                o = o + acc * weight
            obuf[r0:r0 + BLOCK, pair * LANES:(pair + 1) * LANES] = o

    scores = score_group(groups[0])
    for g, group in enumerate(groups):
        upcoming = score_group(groups[g + 1]) if g + 1 < len(groups) else None
        finish_group(group, scores)
        scores = upcoming

    for j in range(rows // BLOCK):
        o = obuf[j * BLOCK:(j + 1) * BLOCK, :]
        scale = _rms_scale(jnp.sum(o * o, axis=1, keepdims=True), B_WIDTH)
        y_ref[j * BLOCK:(j + 1) * BLOCK, :] = (o * scale * gain_ref[...]).astype(BF16)


def _attn_b_call(qkv_b, sinks, gain, w_down):
    seq = qkv_b.shape[0]
    rows = B_ROWS
    blocks_per_tile = rows // BLOCK
    down_rows = D_FF // (seq // rows)
    k_col = B_WIDTH // B_KV_WIDTH
    v_col = k_col + 1

    def kv_tile(col):
        return pl.BlockSpec((rows, B_KV_WIDTH), lambda t: (t, col))

    def kv_halo(col):
        return pl.BlockSpec((BLOCK, B_KV_WIDTH),
                            lambda t: (jnp.maximum(t * blocks_per_tile - 1, 0), col))

    kv_scratch = pltpu.VMEM((B_KV_HEADS, rows + BLOCK, LANES), BF16)
    vmem = (2 * 2 * rows * B_WIDTH * 2 + 4 * 2 * (rows + BLOCK) * LANES * 2
            + 4 * B_KV_HEADS * (rows + BLOCK) * LANES * 2 + rows * B_WIDTH * 4
            + 4 * (rows + BLOCK) * LANES * 4
            + 2 * down_rows * D_MODEL * 6)
    return pl.pallas_call(
        functools.partial(_attn_b_kernel, rows=rows),
        grid=(seq // rows,),
        in_specs=[
            pl.BlockSpec(memory_space=pltpu.SMEM),
            pl.BlockSpec((rows, B_WIDTH), lambda t: (t, 0)),
            kv_tile(k_col), kv_halo(k_col), kv_tile(v_col), kv_halo(v_col),
            pl.BlockSpec((1, B_WIDTH), lambda t: (0, 0)),
            pl.BlockSpec((down_rows, D_MODEL), lambda t: (t, 0)),
        ],
        out_specs=[
            pl.BlockSpec((rows, B_WIDTH), lambda t: (t, 0)),
            pl.BlockSpec((down_rows, D_MODEL), lambda t: (t, 0)),
        ],
        out_shape=[
            jax.ShapeDtypeStruct((seq, B_WIDTH), BF16),
            jax.ShapeDtypeStruct((D_FF, D_MODEL), BF16),
        ],
        scratch_shapes=[kv_scratch, kv_scratch, kv_scratch, kv_scratch,
                        pltpu.VMEM((rows, B_WIDTH), F32)],
        compiler_params=_params(("arbitrary",), vmem, result_bytes=seq * B_WIDTH * 2),
        name="attn_b",
    )(sinks, qkv_b, qkv_b, qkv_b, qkv_b, qkv_b, gain.reshape(1, B_WIDTH), w_down)


def _layer_norm_rows(z, g, b):
    mu = jnp.mean(z, axis=-1, keepdims=True)
    zc = z - mu
    var = jnp.mean(zc * zc, axis=-1, keepdims=True)
    return zc * lax.rsqrt(var + LN_EPS) * g + b


def _out_kernel(ya_ref, yb_ref, ga_ref, w_ref, x_ref, g_ref, b_ref, h_ref):
    blocks = [slice(r, r + OUT_SUB_ROWS) for r in range(0, ya_ref.shape[0], OUT_SUB_ROWS)]
    mixes = []
    for rows in blocks:
        ya = ya_ref[rows, :].astype(F32)
        scale = _rms_scale(jnp.sum(ya * ya, axis=1, keepdims=True), A_WIDTH)
        ya = (ya * scale * ga_ref[...]).astype(BF16)
        mix = jnp.dot(ya, w_ref[0:A_WIDTH, :], preferred_element_type=F32)
        mixes.append(mix + jnp.dot(yb_ref[rows, :], w_ref[A_WIDTH:, :], preferred_element_type=F32))
    for rows, mix in zip(blocks, mixes):
        h_ref[rows, :] = _layer_norm_rows(ALPHA * x_ref[rows, :] + mix, g_ref[...], b_ref[...])


def _out_call(ya, yb, gain_a, w_out_bf, x2d, ln_g, ln_b):
    seq = x2d.shape[0]
    tm = OUT_ROWS
    row_spec = pl.BlockSpec((1, D_MODEL), lambda i: (0, 0))
    vmem = (D_MODEL * D_MODEL * 2 + 2 * 2 * tm * A_WIDTH * 2
            + 2 * 2 * tm * D_MODEL * 4 + 2 * tm * D_MODEL * 4)
    return pl.pallas_call(
        _out_kernel,
        grid=(seq // tm,),
        in_specs=[
            pl.BlockSpec((tm, A_WIDTH), lambda i: (i, 0)),
            pl.BlockSpec((tm, B_WIDTH), lambda i: (i, 0)),
            pl.BlockSpec((1, A_WIDTH), lambda i: (0, 0)),
            pl.BlockSpec((D_MODEL, D_MODEL), lambda i: (0, 0), pipeline_mode=pl.Buffered(1)),
            pl.BlockSpec((tm, D_MODEL), lambda i: (i, 0)),
            row_spec, row_spec,
        ],
        out_specs=pl.BlockSpec((tm, D_MODEL), lambda i: (i, 0)),
        out_shape=jax.ShapeDtypeStruct((seq, D_MODEL), F32),
        compiler_params=_params(("arbitrary",), vmem),
        name="out_proj_ln",
    )(ya, yb, gain_a.reshape(1, A_WIDTH), w_out_bf, x2d,
      ln_g.reshape(1, D_MODEL), ln_b.reshape(1, D_MODEL))


def _ffn_kernel(h_ref, wg_ref, wu_ref, wd_ref, g_ref, b_ref, o_ref, hb_ref):
    j = pl.program_id(1)
    last = pl.num_programs(1) - 1
    half = wg_ref.shape[1] // 2
    halves = [slice(0, half), slice(half, 2 * half)]

    def activations(hb):
        gates = [jnp.dot(hb, wg_ref[:, s], preferred_element_type=F32) for s in halves]
        ups = [jnp.dot(hb, wu_ref[:, s], preferred_element_type=F32) for s in halves]
        return [(g * jax.nn.sigmoid(g) * u).astype(BF16) for g, u in zip(gates, ups)]

    def down(acts, rows=slice(None)):
        return (jnp.dot(acts[0][rows], wd_ref[halves[0], :], preferred_element_type=F32)
                + jnp.dot(acts[1][rows], wd_ref[halves[1], :], preferred_element_type=F32))

    @pl.when(j == 0)
    def _():
        h = h_ref[...]
        hb = h.astype(BF16)
        hb_ref[...] = hb
        o_ref[...] = ALPHA * h + down(activations(hb))

    @pl.when((j > 0) & (j < last))
    def _():
        o_ref[...] += down(activations(hb_ref[...]))

    @pl.when(j == last)
    def _():
        acts = activations(hb_ref[...])
        for r in range(0, o_ref.shape[0], FFN_NORM_ROWS):
            rows = slice(r, r + FFN_NORM_ROWS)
            o_ref[rows, :] = _layer_norm_rows(o_ref[rows, :] + down(acts, rows),
                                              g_ref[...], b_ref[...])


def _ffn_call(h, w_up_b, w_down_b, ln_g, ln_b):
    seq = h.shape[0]
    tm, tf = FFN_ROWS, FFN_CHUNK
    n_chunks = D_FF // tf
    assert n_chunks >= 2
    row_spec = pl.BlockSpec((1, D_MODEL), lambda i, j: (0, 0))
    vmem = (2 * tm * D_MODEL * 4 + tm * D_MODEL * 2 + 2 * tm * D_MODEL * 4
            + 2 * 3 * D_MODEL * tf * 2 + 3 * tm * tf * 4)
    return pl.pallas_call(
        _ffn_kernel,
        grid=(seq // tm, n_chunks),
        in_specs=[
            pl.BlockSpec((tm, D_MODEL), lambda i, j: (i, 0)),
            pl.BlockSpec((D_MODEL, tf), lambda i, j: (0, j)),
            pl.BlockSpec((D_MODEL, tf), lambda i, j: (0, n_chunks + j)),
            pl.BlockSpec((tf, D_MODEL), lambda i, j: (j, 0)),
            row_spec, row_spec,
        ],
        out_specs=pl.BlockSpec((tm, D_MODEL), lambda i, j: (i, 0)),
        out_shape=jax.ShapeDtypeStruct((seq, D_MODEL), F32),
        scratch_shapes=[pltpu.VMEM((tm, D_MODEL), BF16)],
        compiler_params=_params(("arbitrary", "arbitrary"), vmem),
        name="swiglu_ln",
    )(h, w_up_b, w_up_b, w_down_b, ln_g.reshape(1, D_MODEL), ln_b.reshape(1, D_MODEL))


def _layer(x2d, w_in, b_in, sinks, g_mix_a, g_mix_b, w_out, ln1_g, ln1_b, w_up, w_down, ln2_g, ln2_b):
    qkv4, qkv16, qkv_b, w_out_b, w_up_b = _proj_call(x2d, w_in, b_in, w_out, w_up)
    ya = _attn_a_call(qkv4, qkv16)
    yb, w_down_b = _attn_b_call(qkv_b, sinks, g_mix_b, w_down)
    h = _out_call(ya, yb, g_mix_a, w_out_b, x2d, ln1_g, ln1_b)
    return _ffn_call(h, w_up_b, w_down_b, ln2_g, ln2_b)


def kernel(x, w_in, b_in, sinks, g_mix_a, g_mix_b, w_out, ln1_g, ln1_b, w_up, w_down, ln2_g, ln2_b):
    batch, seq, d_model = x.shape
    assert (batch, d_model) == (1, D_MODEL) and seq % A_TILE == 0
    h = x.reshape(seq, d_model)
    for i in range(w_in.shape[0]):
        h = _layer(h, w_in[i], b_in[i], sinks[i], g_mix_a[i], g_mix_b[i], w_out[i],
                   ln1_g[i], ln1_b[i], w_up[i], w_down[i], ln2_g[i], ln2_b[i])
    return h.reshape(batch, seq, d_model)
```

```python
import functools
import math

import jax
import jax.numpy as jnp
from jax import lax
from jax.experimental import pallas as pl
from jax.experimental.pallas import tpu as pltpu

F32 = jnp.float32
BF16 = jnp.bfloat16

D_MODEL = 2048
A_HEAD_DIM = 128
A_WIDTH = 1024
A_HEADS = 8
MID_DIL = 4
FAR_DIL = 16
B_HEAD_DIM = 64
B_WIDTH = 1024
B_HEADS = 16
B_KV_HEADS = 2
B_GROUP = 8
B_KV_WIDTH = 128
QKV_A_WIDTH = 3 * A_WIDTH
QKV_B_WIDTH = B_WIDTH + 2 * B_KV_WIDTH
IN_WIDTH = QKV_A_WIDTH + QKV_B_WIDTH
D_FF = 5632
BLOCK = 128
ROPE_THETA = 10000.0
ALPHA = 2.0 ** 0.25
LN_EPS = 1e-5
RMS_EPS = 1e-6
LOG2_E = math.log2(math.e)

LANES = 128
V7X_VMEM_BYTES = 64 * 1024 * 1024
V7X_VMEM_REQUEST_CAP = V7X_VMEM_BYTES - 2 * 1024 * 1024

PROJ_ROWS = 512
PROJ_CHUNK = 512
A_TILE = FAR_DIL * BLOCK
B_ROWS = 1024
OUT_ROWS = 512
OUT_SUB_ROWS = 256
FFN_ROWS = 1024
FFN_CHUNK = 512
FFN_NORM_ROWS = 256

A_MID_ROWS = A_TILE // MID_DIL
NEAR_SLAB = BLOCK // MID_DIL
A_STEP_HEADS = 2
GROUP = 4
GROUPS_AHEAD = 1
B_ITEM_GROUP = 2


def _vmem_limit(estimate_bytes, result_bytes=0):
    request = estimate_bytes + estimate_bytes // 4
    if result_bytes:
        request = max(request, V7X_VMEM_BYTES - result_bytes + result_bytes // 8)
    return int(min(V7X_VMEM_REQUEST_CAP, request))


def _params(semantics, vmem_estimate, result_bytes=0):
    return pltpu.CompilerParams(dimension_semantics=semantics,
                                vmem_limit_bytes=_vmem_limit(vmem_estimate, result_bytes))


def _rope_factors(seq, tile, head_dim):
    half = head_dim // 2
    inv_freq = ROPE_THETA ** (-jnp.arange(half, dtype=F32) / half)
    reps = LANES // head_dim

    def expand(pos):
        ang = pos.astype(F32)[:, None] * inv_freq[None, :]
        cos = jnp.cos(ang)
        sin = jnp.sin(ang)
        return (jnp.tile(jnp.concatenate([cos, cos], axis=-1), (1, reps)),
                jnp.tile(jnp.concatenate([sin, sin], axis=-1), (1, reps)))

    return expand(jnp.arange(tile)) + expand(jnp.arange(seq // tile) * tile)


def _proj_kernel(x_ref, w_ref, b_ref, oca_ref, osa_ref, tca_ref, tsa_ref,
                 ocb_ref, osb_ref, tcb_ref, tsb_ref, wout_ref, wup_ref,
                 o4_ref, o16_ref, ob_ref, wout_bf_ref, wup_bf_ref, nat_buf, mid_buf):
    rows = x_ref.shape[0]
    step = pl.program_id(0)
    wout_bf_ref[...] = wout_ref[...].astype(BF16)
    wup_bf_ref[...] = wup_ref[...].astype(BF16)
    xb = x_ref[...].astype(BF16)
    lane = lax.broadcasted_iota(jnp.int32, (rows, LANES), 1)
    first_half_a = lane < (A_HEAD_DIM // 2)
    first_half_b = (lane % B_HEAD_DIM) < (B_HEAD_DIM // 2)

    def tables(off_c_ref, off_s_ref, start_c_ref, start_s_ref, first_half):
        oc, os_ = off_c_ref[...], off_s_ref[...]
        tc, ts = start_c_ref[pl.ds(step, 1), :], start_s_ref[pl.ds(step, 1), :]
        sin = ts * oc + tc * os_
        return tc * oc - ts * os_, jnp.where(first_half, -sin, sin)

    ca, sa = tables(oca_ref, osa_ref, tca_ref, tsa_ref, first_half_a)
    cb, sb = tables(ocb_ref, osb_ref, tcb_ref, tsb_ref, first_half_b)
    scale_a = LOG2_E / math.sqrt(A_HEAD_DIM)
    scale_b = LOG2_E / math.sqrt(B_HEAD_DIM)
    mid_rows = rows // MID_DIL
    far_rows = rows // FAR_DIL

    def rope_a(y):
        return y * ca + pltpu.roll(y, A_HEAD_DIM // 2, 1) * sa

    def rope_b(y):
        partner = jnp.where(first_half_b,
                            pltpu.roll(y, LANES - B_HEAD_DIM // 2, 1),
                            pltpu.roll(y, B_HEAD_DIM // 2, 1))
        return y * cb + partner * sb

    starts = list(range(0, IN_WIDTH, PROJ_CHUNK))
    b_query = [c for c in starts if QKV_A_WIDTH <= c < QKV_A_WIDTH + B_WIDTH]
    for c0 in b_query + [c for c in starts if c not in b_query]:
        c1 = min(c0 + PROJ_CHUNK, IN_WIDTH)
        acc = jnp.dot(xb, w_ref[:, c0:c1], preferred_element_type=F32) + b_ref[:, c0:c1]
        for g, col in enumerate(range(c0, c1, LANES)):
            y = acc[:, col - c0:col - c0 + LANES]
            if col < A_WIDTH:
                y = rope_a(y) * scale_a
            elif col < 2 * A_WIDTH:
                y = rope_a(y)
            elif col < QKV_A_WIDTH:
                pass
            elif col < QKV_A_WIDTH + B_WIDTH:
                y = rope_b(y) * scale_b
            elif col < QKV_A_WIDTH + B_WIDTH + B_KV_WIDTH:
                y = rope_b(y)
            if col >= QKV_A_WIDTH:
                ob_ref[:, col - QKV_A_WIDTH:col - QKV_A_WIDTH + LANES] = y.astype(BF16)
                continue
            nat_buf[g] = y
            for c in range(MID_DIL):
                cls = nat_buf[g, pl.ds(c, mid_rows, stride=MID_DIL), :]
                o4_ref[c, :, col:col + LANES] = cls.astype(BF16)
                mid_buf[g, c * mid_rows:(c + 1) * mid_rows, :] = cls
            for c in range(MID_DIL):
                for c2 in range(FAR_DIL // MID_DIL):
                    cls = mid_buf[g, pl.ds(c * mid_rows + c2, far_rows, stride=MID_DIL), :]
                    o16_ref[c + MID_DIL * c2, :, col:col + LANES] = cls.astype(BF16)


def _proj_call(x2d, w_in, b_in, w_out, w_up):
    seq = x2d.shape[0]
    tm = PROJ_ROWS
    steps = seq // tm
    rope_a = _rope_factors(seq, tm, A_HEAD_DIM)
    rope_b = _rope_factors(seq, tm, B_HEAD_DIM)
    offset_spec = pl.BlockSpec((tm, LANES), lambda i: (0, 0))
    start_spec = pl.BlockSpec((steps, LANES), lambda i: (0, 0))
    rope_specs = [offset_spec, offset_spec, start_spec, start_spec]
    wout_rows = D_MODEL // steps
    groups = PROJ_CHUNK // LANES
    vmem = (D_MODEL * IN_WIDTH * 2
            + 2 * tm * D_MODEL * 4
            + 2 * tm * (2 * QKV_A_WIDTH + QKV_B_WIDTH) * 2
            + 2 * 4 * (tm + steps) * LANES * 4
            + 2 * wout_rows * (D_MODEL + 2 * D_FF) * 6
            + 2 * groups * tm * LANES * 4
            + tm * D_MODEL * 2 + 2 * tm * PROJ_CHUNK * 4)
    return pl.pallas_call(
        _proj_kernel,
        grid=(steps,),
        in_specs=[
            pl.BlockSpec((tm, D_MODEL), lambda i: (i, 0)),
            pl.BlockSpec((D_MODEL, IN_WIDTH), lambda i: (0, 0), pipeline_mode=pl.Buffered(1)),
            pl.BlockSpec((1, IN_WIDTH), lambda i: (0, 0)),
            *rope_specs, *rope_specs,
            pl.BlockSpec((wout_rows, D_MODEL), lambda i: (i, 0)),
            pl.BlockSpec((wout_rows, 2 * D_FF), lambda i: (i, 0)),
        ],
        out_specs=[
            pl.BlockSpec((MID_DIL, tm // MID_DIL, QKV_A_WIDTH), lambda i: (0, i, 0)),
            pl.BlockSpec((FAR_DIL, tm // FAR_DIL, QKV_A_WIDTH), lambda i: (0, i, 0)),
            pl.BlockSpec((tm, QKV_B_WIDTH), lambda i: (i, 0)),
            pl.BlockSpec((wout_rows, D_MODEL), lambda i: (i, 0)),
            pl.BlockSpec((wout_rows, 2 * D_FF), lambda i: (i, 0)),
        ],
        out_shape=[
            jax.ShapeDtypeStruct((MID_DIL, seq // MID_DIL, QKV_A_WIDTH), BF16),
            jax.ShapeDtypeStruct((FAR_DIL, seq // FAR_DIL, QKV_A_WIDTH), BF16),
            jax.ShapeDtypeStruct((seq, QKV_B_WIDTH), BF16),
            jax.ShapeDtypeStruct((D_MODEL, D_MODEL), BF16),
            jax.ShapeDtypeStruct((D_MODEL, 2 * D_FF), BF16),
        ],
        scratch_shapes=[pltpu.VMEM((groups, tm, LANES), F32), pltpu.VMEM((groups, tm, LANES), F32)],
        compiler_params=_params(("arbitrary",), vmem),
        name="proj_rope",
    )(x2d, w_in.astype(BF16), b_in.reshape(1, IN_WIDTH), *rope_a, *rope_b, w_out, w_up)


def _band_masks(lo, is_first_tile):
    qi = lax.broadcasted_iota(jnp.int32, (BLOCK, 2 * BLOCK), 0)
    kj = lax.broadcasted_iota(jnp.int32, (BLOCK, 2 * BLOCK), 1)
    band = (kj >= qi + lo) & (kj <= qi + BLOCK)
    first_lower = jnp.where(is_first_tile, BLOCK, 0)
    return band, band & (kj >= first_lower)


def _near_masks(is_first_tile):
    iq = lax.broadcasted_iota(jnp.int32, (BLOCK, 2 * BLOCK), 0)
    ik = lax.broadcasted_iota(jnp.int32, (BLOCK, 2 * BLOCK), 1)
    cq, mq = iq // NEAR_SLAB, iq % NEAR_SLAB
    ck, mk = ik // (2 * NEAR_SLAB), ik % (2 * NEAR_SLAB)
    dist = MID_DIL * (mq + NEAR_SLAB - mk) + (cq - ck)
    valid = (dist >= 0) & (dist <= BLOCK)
    first_lower = jnp.where(is_first_tile, NEAR_SLAB, 0)
    return valid, valid & (mk >= first_lower)


def _rms_scale(sum_sq, width):
    return lax.rsqrt(sum_sq * (1.0 / width) + RMS_EPS)


def _qk(q, k):
    return lax.dot_general(q, k, (((1,), (1,)), ((), ())), preferred_element_type=F32)


def _attn_a_kernel(q4_ref, k4_ref, k4h_ref, v4_ref, v4h_ref,
                   q16_ref, k16_ref, k16h_ref, v16_ref, v16h_ref, y_ref,
                   k4buf, v4buf, k16buf, v16buf, *state_refs):
    first = pl.program_id(0) == 0
    k4buf[:, 0:BLOCK, :] = k4h_ref[...]
    k4buf[:, BLOCK:, :] = k4_ref[...]
    v4buf[:, 0:BLOCK, :] = v4h_ref[...]
    v4buf[:, BLOCK:, :] = v4_ref[...]
    k16buf[:, 0:BLOCK, :] = k16h_ref[...]
    k16buf[:, BLOCK:, :] = k16_ref[...]
    v16buf[:, 0:BLOCK, :] = v16h_ref[...]
    v16buf[:, BLOCK:, :] = v16_ref[...]

    near, near_first = _near_masks(first)
    band, band_first = _band_masks(0, first)
    full = (BLOCK, LANES)
    ones = jnp.ones((2 * BLOCK, LANES), BF16)

    def head_items(hd):
        hs = slice(hd * A_HEAD_DIM, (hd + 1) * A_HEAD_DIM)
        acc_f, m_f, l_f, acc_m, m_m, l_m, y_s = state_refs[7 * hd:7 * hd + 7]

        def far_item(c, c2):
            r = c + MID_DIL * c2
            rows = pl.ds(c * A_MID_ROWS + c2, BLOCK, stride=MID_DIL)

            def store(m, l, acc):
                acc_f[rows, :] = acc
                m_f[rows, :] = m
                l_f[rows, :] = l

            return (lambda: q16_ref[r, :, hs], lambda: k16buf[r, :, hs], lambda: v16buf[r, :, hs],
                    band_first, None, store)

        def mid_item(c, j):
            rows = slice(c * A_MID_ROWS + j * BLOCK, c * A_MID_ROWS + (j + 1) * BLOCK)

            def store(m, l, acc):
                acc_m[rows, :] = acc
                m_m[rows, :] = m
                l_m[rows, :] = l

            return (lambda: q4_ref[c, j * BLOCK:(j + 1) * BLOCK, hs],
                    lambda: k4buf[c, j * BLOCK:(j + 2) * BLOCK, hs],
                    lambda: v4buf[c, j * BLOCK:(j + 2) * BLOCK, hs],
                    band_first if j == 0 else band,
                    lambda: (m_f[rows, :], l_f[rows, :], acc_f[rows, :]), store)

        def near_item(j):
            r0 = j * NEAR_SLAB
            h0 = BLOCK + r0 - NEAR_SLAB

            def gather(ref, start, size):
                return jnp.concatenate([ref[c, start:start + size, hs] for c in range(MID_DIL)],
                                       axis=0)

            def state(buf):
                return jnp.concatenate([buf[c * A_MID_ROWS + r0:c * A_MID_ROWS + r0 + NEAR_SLAB, :]
                                        for c in range(MID_DIL)], axis=0)

            def store(m, l, acc):
                out = acc * (1.0 / l)
                for c in range(MID_DIL):
                    y_s[pl.ds(j * BLOCK + c, NEAR_SLAB, stride=MID_DIL), :] = (
                        out[c * NEAR_SLAB:(c + 1) * NEAR_SLAB])

            return (lambda: gather(q4_ref, r0, NEAR_SLAB), lambda: gather(k4buf, h0, 2 * NEAR_SLAB),
                    lambda: gather(v4buf, h0, 2 * NEAR_SLAB), near_first if j == 0 else near,
                    lambda: (state(m_m), state(l_m), state(acc_m)), store)

        items = [far_item(c, c2) for c in range(MID_DIL) for c2 in range(FAR_DIL // MID_DIL)]
        items += [mid_item(c, j) for c in range(MID_DIL) for j in range(A_MID_ROWS // BLOCK)]
        items += [near_item(j) for j in range(A_TILE // BLOCK)]
        return items

    work = [item for hd in range(A_STEP_HEADS) for item in head_items(hd)]
    groups = [work[g:g + GROUP] for g in range(0, len(work), GROUP)]

    def score_group(items):
        return [_qk(q(), k()) for q, k, _, _, _, _ in items]

    def finish_group(items, scores):
        staged = []
        for (_, _, _, valid, old, _), s in zip(items, scores):
            s = jnp.where(valid, s, -jnp.inf)
            m = jnp.max(s, axis=1, keepdims=True)
            if old is None:
                staged.append((jnp.broadcast_to(m, full), jnp.exp2(s - m).astype(BF16), None))
                continue
            m_old, l_old, acc_old = old()
            m = jnp.maximum(m_old, m)
            p = jnp.concatenate([jnp.exp2(s[:, :LANES] - m), jnp.exp2(s[:, LANES:] - m)], axis=1)
            staged.append((m, p.astype(BF16), (jnp.exp2(m_old - m), l_old, acc_old)))
        for (_, _, v, _, _, store), (m, p, old) in zip(items, staged):
            pv = jnp.dot(p, jnp.concatenate([v(), ones], axis=1), preferred_element_type=F32)
            acc, l = pv[:, :LANES], pv[:, LANES:]
            if old is not None:
                rescale, l_old, acc_old = old
                acc = rescale * acc_old + acc
                l = rescale * l_old + l
            store(m, l, acc)

    pending = [score_group(items) for items in groups[:GROUPS_AHEAD]]
    for g, items in enumerate(groups):
        if g + GROUPS_AHEAD < len(groups):
            pending.append(score_group(groups[g + GROUPS_AHEAD]))
        finish_group(items, pending.pop(0))

    for hd in range(A_STEP_HEADS):
        y_ref[:, hd * A_HEAD_DIM:(hd + 1) * A_HEAD_DIM] = state_refs[7 * hd + 6][...].astype(BF16)


def _attn_a_call(qkv4, qkv16):
    seq = qkv4.shape[0] * qkv4.shape[1]
    tiles = seq // A_TILE
    mid_blocks = A_MID_ROWS // BLOCK

    width = A_STEP_HEADS * A_HEAD_DIM
    col_steps = A_HEADS // A_STEP_HEADS

    def mid_tile(which):
        return pl.BlockSpec((MID_DIL, A_MID_ROWS, width),
                            lambda t, h: (0, t, which * col_steps + h))

    def mid_halo(which):
        return pl.BlockSpec((MID_DIL, BLOCK, width),
                            lambda t, h: (0, jnp.maximum(t * mid_blocks - 1, 0), which * col_steps + h))

    def far_tile(which):
        return pl.BlockSpec((FAR_DIL, BLOCK, width), lambda t, h: (0, t, which * col_steps + h))

    def far_halo(which):
        return pl.BlockSpec((FAR_DIL, BLOCK, width),
                            lambda t, h: (0, jnp.maximum(t - 1, 0), which * col_steps + h))

    slab = A_TILE * width
    vmem = (2 * (3 * slab + 2 * slab // 4) * 2
            + 2 * 5 * slab * 2
            + 2 * slab * 2
            + 2 * (slab + slab // 4) * 2 + 2 * 2 * slab * 2
            + 7 * slab * 4)
    return pl.pallas_call(
        _attn_a_kernel,
        grid=(tiles, col_steps),
        in_specs=[mid_tile(0), mid_tile(1), mid_halo(1), mid_tile(2), mid_halo(2),
                  far_tile(0), far_tile(1), far_halo(1), far_tile(2), far_halo(2)],
        out_specs=pl.BlockSpec((A_TILE, width), lambda t, h: (t, h)),
        out_shape=jax.ShapeDtypeStruct((seq, A_WIDTH), BF16),
        scratch_shapes=[
            pltpu.VMEM((MID_DIL, A_MID_ROWS + BLOCK, width), BF16),
            pltpu.VMEM((MID_DIL, A_MID_ROWS + BLOCK, width), BF16),
            pltpu.VMEM((FAR_DIL, 2 * BLOCK, width), BF16),
            pltpu.VMEM((FAR_DIL, 2 * BLOCK, width), BF16),
        ] + [pltpu.VMEM((A_TILE, LANES), F32)] * (7 * A_STEP_HEADS),
        compiler_params=_params(("arbitrary", "arbitrary"), vmem, result_bytes=seq * A_WIDTH * 2),
        name="attn_a",
    )(qkv4, qkv4, qkv4, qkv4, qkv4, qkv16, qkv16, qkv16, qkv16, qkv16)


def _attn_b_kernel(sinks_ref, q_ref, km_ref, kh_ref, vm_ref, vh_ref, gain_ref, wdn_ref,
                   y_ref, wdn_bf_ref, klo, khi, vlo, vhi, obuf, *, rows):
    wdn_bf_ref[...] = wdn_ref[...].astype(BF16)
    lane = lax.broadcasted_iota(jnp.int32, (rows + BLOCK, LANES), 1)
    low = lane < B_HEAD_DIM

    def fill(dst_lo, dst_hi, halo_ref, main_ref):
        full = jnp.concatenate([halo_ref[...], main_ref[...]], axis=0).astype(F32)
        swapped = pltpu.roll(full, B_HEAD_DIM, 1)
        zero = jnp.zeros_like(full)
        for c in range(B_KV_HEADS):
            own_low, own_high = (full, swapped) if c == 0 else (swapped, full)
            dst_lo[c] = jnp.where(low, own_low, zero).astype(BF16)
            dst_hi[c] = jnp.where(low, zero, own_high).astype(BF16)

    fill(klo, khi, kh_ref, km_ref)
    fill(vlo, vhi, vh_ref, vm_ref)

    band, band_first = _band_masks(1, pl.program_id(0) == 0)

    items = [(j, pair) for j in range(rows // BLOCK) for pair in range(B_HEADS // 2)]
    groups = [items[g:g + B_ITEM_GROUP] for g in range(0, len(items), B_ITEM_GROUP)]

    col0 = lax.broadcasted_iota(jnp.int32, (1, LANES), 1) == 0
    neg_inf_row = jnp.full((1, LANES), -jnp.inf, F32)

    def score_group(group):
        out = []
        for j, pair in group:
            c = (2 * pair) // B_GROUP
            r0 = j * BLOCK
            q = q_ref[r0:r0 + BLOCK, pair * LANES:(pair + 1) * LANES]
            valid = band_first if j == 0 else band
            pair_scores = []
            for odd, k_sel in enumerate((klo, khi)):
                s = _qk(q, k_sel[c, r0:r0 + 2 * BLOCK, :])
                sink_row = jnp.where(col0, sinks_ref[2 * pair + odd] * LOG2_E, neg_inf_row)
                pair_scores.append(jnp.concatenate(
                    [jnp.where(valid[:, :LANES], s[:, :LANES], sink_row),
                     jnp.where(valid[:, LANES:], s[:, LANES:], -jnp.inf)], axis=1))
            out.append(pair_scores)
        return out

    def finish_group(group, scores):
        staged = []
        for (j, pair), pair_scores in zip(group, scores):
            for odd, s in enumerate(pair_scores):
                m = jnp.max(s, axis=1, keepdims=True)
                p = jnp.exp2(s - m)
                l = jnp.sum(p, axis=1, keepdims=True)
                p = jnp.concatenate([jnp.where(col0, 0.0, p[:, :LANES]), p[:, LANES:]], axis=1)
                staged.append((p.astype(BF16), 1.0 / l))
        for i, (j, pair) in enumerate(group):
            c = (2 * pair) // B_GROUP
            r0 = j * BLOCK
            o = jnp.zeros((BLOCK, LANES), F32)
            for odd, v_sel in enumerate((vlo, vhi)):
                p, weight = staged[2 * i + odd]
                acc = jnp.dot(p, v_sel[c, r0:r0 + 2 * BLOCK, :], preferred_element_type=F32)
                o = o + acc * weight
            obuf[r0:r0 + BLOCK, pair * LANES:(pair + 1) * LANES] = o

    scores = score_group(groups[0])
    for g, group in enumerate(groups):
        upcoming = score_group(groups[g + 1]) if g + 1 < len(groups) else None
        finish_group(group, scores)
        scores = upcoming

    for j in range(rows // BLOCK):
        o = obuf[j * BLOCK:(j + 1) * BLOCK, :]
        scale = _rms_scale(jnp.sum(o * o, axis=1, keepdims=True), B_WIDTH)
        y_ref[j * BLOCK:(j + 1) * BLOCK, :] = (o * scale * gain_ref[...]).astype(BF16)


def _attn_b_call(qkv_b, sinks, gain, w_down):
    seq = qkv_b.shape[0]
    rows = B_ROWS
    blocks_per_tile = rows // BLOCK
    down_rows = D_FF // (seq // rows)
    k_col = B_WIDTH // B_KV_WIDTH
    v_col = k_col + 1

    def kv_tile(col):
        return pl.BlockSpec((rows, B_KV_WIDTH), lambda t: (t, col))

    def kv_halo(col):
        return pl.BlockSpec((BLOCK, B_KV_WIDTH),
                            lambda t: (jnp.maximum(t * blocks_per_tile - 1, 0), col))

    kv_scratch = pltpu.VMEM((B_KV_HEADS, rows + BLOCK, LANES), BF16)
    vmem = (2 * 2 * rows * B_WIDTH * 2 + 4 * 2 * (rows + BLOCK) * LANES * 2
            + 4 * B_KV_HEADS * (rows + BLOCK) * LANES * 2 + rows * B_WIDTH * 4
            + 4 * (rows + BLOCK) * LANES * 4
            + 2 * down_rows * D_MODEL * 6)
    return pl.pallas_call(
        functools.partial(_attn_b_kernel, rows=rows),
        grid=(seq // rows,),
        in_specs=[
            pl.BlockSpec(memory_space=pltpu.SMEM),
            pl.BlockSpec((rows, B_WIDTH), lambda t: (t, 0)),
            kv_tile(k_col), kv_halo(k_col), kv_tile(v_col), kv_halo(v_col),
            pl.BlockSpec((1, B_WIDTH), lambda t: (0, 0)),
            pl.BlockSpec((down_rows, D_MODEL), lambda t: (t, 0)),
        ],
        out_specs=[
            pl.BlockSpec((rows, B_WIDTH), lambda t: (t, 0)),
            pl.BlockSpec((down_rows, D_MODEL), lambda t: (t, 0)),
        ],
        out_shape=[
            jax.ShapeDtypeStruct((seq, B_WIDTH), BF16),
            jax.ShapeDtypeStruct((D_FF, D_MODEL), BF16),
        ],
        scratch_shapes=[kv_scratch, kv_scratch, kv_scratch, kv_scratch,
                        pltpu.VMEM((rows, B_WIDTH), F32)],
        compiler_params=_params(("arbitrary",), vmem, result_bytes=seq * B_WIDTH * 2),
        name="attn_b",
    )(sinks, qkv_b, qkv_b, qkv_b, qkv_b, qkv_b, gain.reshape(1, B_WIDTH), w_down)


def _layer_norm_rows(z, g, b):
    mu = jnp.mean(z, axis=-1, keepdims=True)
    zc = z - mu
    var = jnp.mean(zc * zc, axis=-1, keepdims=True)
    return zc * lax.rsqrt(var + LN_EPS) * g + b


def _out_kernel(ya_ref, yb_ref, ga_ref, w_ref, x_ref, g_ref, b_ref, h_ref):
    blocks = [slice(r, r + OUT_SUB_ROWS) for r in range(0, ya_ref.shape[0], OUT_SUB_ROWS)]
    mixes = []
    for rows in blocks:
        ya = ya_ref[rows, :].astype(F32)
        scale = _rms_scale(jnp.sum(ya * ya, axis=1, keepdims=True), A_WIDTH)
        ya = (ya * scale * ga_ref[...]).astype(BF16)
        mix = jnp.dot(ya, w_ref[0:A_WIDTH, :], preferred_element_type=F32)
        mixes.append(mix + jnp.dot(yb_ref[rows, :], w_ref[A_WIDTH:, :], preferred_element_type=F32))
    for rows, mix in zip(blocks, mixes):
        h_ref[rows, :] = _layer_norm_rows(ALPHA * x_ref[rows, :] + mix, g_ref[...], b_ref[...])


def _out_call(ya, yb, gain_a, w_out_bf, x2d, ln_g, ln_b):
    seq = x2d.shape[0]
    tm = OUT_ROWS
    row_spec = pl.BlockSpec((1, D_MODEL), lambda i: (0, 0))
    vmem = (D_MODEL * D_MODEL * 2 + 2 * 2 * tm * A_WIDTH * 2
            + 2 * 2 * tm * D_MODEL * 4 + 2 * tm * D_MODEL * 4)
    return pl.pallas_call(
        _out_kernel,
        grid=(seq // tm,),
        in_specs=[
            pl.BlockSpec((tm, A_WIDTH), lambda i: (i, 0)),
            pl.BlockSpec((tm, B_WIDTH), lambda i: (i, 0)),
            pl.BlockSpec((1, A_WIDTH), lambda i: (0, 0)),
            pl.BlockSpec((D_MODEL, D_MODEL), lambda i: (0, 0), pipeline_mode=pl.Buffered(1)),
            pl.BlockSpec((tm, D_MODEL), lambda i: (i, 0)),
            row_spec, row_spec,
        ],
        out_specs=pl.BlockSpec((tm, D_MODEL), lambda i: (i, 0)),
        out_shape=jax.ShapeDtypeStruct((seq, D_MODEL), F32),
        compiler_params=_params(("arbitrary",), vmem),
        name="out_proj_ln",
    )(ya, yb, gain_a.reshape(1, A_WIDTH), w_out_bf, x2d,
      ln_g.reshape(1, D_MODEL), ln_b.reshape(1, D_MODEL))


def _ffn_kernel(h_ref, wg_ref, wu_ref, wd_ref, g_ref, b_ref, o_ref, hb_ref):
    j = pl.program_id(1)
    last = pl.num_programs(1) - 1
    half = wg_ref.shape[1] // 2
    halves = [slice(0, half), slice(half, 2 * half)]

    def activations(hb):
        gates = [jnp.dot(hb, wg_ref[:, s], preferred_element_type=F32) for s in halves]
        ups = [jnp.dot(hb, wu_ref[:, s], preferred_element_type=F32) for s in halves]
        return [(g * jax.nn.sigmoid(g) * u).astype(BF16) for g, u in zip(gates, ups)]

    def down(acts, rows=slice(None)):
        return (jnp.dot(acts[0][rows], wd_ref[halves[0], :], preferred_element_type=F32)
                + jnp.dot(acts[1][rows], wd_ref[halves[1], :], preferred_element_type=F32))

    @pl.when(j == 0)
    def _():
        h = h_ref[...]
        hb = h.astype(BF16)
        hb_ref[...] = hb
        o_ref[...] = ALPHA * h + down(activations(hb))

    @pl.when((j > 0) & (j < last))
    def _():
        o_ref[...] += down(activations(hb_ref[...]))

    @pl.when(j == last)
    def _():
        acts = activations(hb_ref[...])
        for r in range(0, o_ref.shape[0], FFN_NORM_ROWS):
            rows = slice(r, r + FFN_NORM_ROWS)
            o_ref[rows, :] = _layer_norm_rows(o_ref[rows, :] + down(acts, rows),
                                              g_ref[...], b_ref[...])


def _ffn_call(h, w_up_b, w_down_b, ln_g, ln_b):
    seq = h.shape[0]
    tm, tf = FFN_ROWS, FFN_CHUNK
    n_chunks = D_FF // tf
    assert n_chunks >= 2
    row_spec = pl.BlockSpec((1, D_MODEL), lambda i, j: (0, 0))
    vmem = (2 * tm * D_MODEL * 4 + tm * D_MODEL * 2 + 2 * tm * D_MODEL * 4
            + 2 * 3 * D_MODEL * tf * 2 + 3 * tm * tf * 4)
    return pl.pallas_call(
        _ffn_kernel,
        grid=(seq // tm, n_chunks),
        in_specs=[
            pl.BlockSpec((tm, D_MODEL), lambda i, j: (i, 0)),
            pl.BlockSpec((D_MODEL, tf), lambda i, j: (0, j)),
            pl.BlockSpec((D_MODEL, tf), lambda i, j: (0, n_chunks + j)),
            pl.BlockSpec((tf, D_MODEL), lambda i, j: (j, 0)),
            row_spec, row_spec,
        ],
        out_specs=pl.BlockSpec((tm, D_MODEL), lambda i, j: (i, 0)),
        out_shape=jax.ShapeDtypeStruct((seq, D_MODEL), F32),
        scratch_shapes=[pltpu.VMEM((tm, D_MODEL), BF16)],
        compiler_params=_params(("arbitrary", "arbitrary"), vmem),
        name="swiglu_ln",
    )(h, w_up_b, w_up_b, w_down_b, ln_g.reshape(1, D_MODEL), ln_b.reshape(1, D_MODEL))


def _layer(x2d, w_in, b_in, sinks, g_mix_a, g_mix_b, w_out, ln1_g, ln1_b, w_up, w_down, ln2_g, ln2_b):
    qkv4, qkv16, qkv_b, w_out_b, w_up_b = _proj_call(x2d, w_in, b_in, w_out, w_up)
    ya = _attn_a_call(qkv4, qkv16)
    yb, w_down_b = _attn_b_call(qkv_b, sinks, g_mix_b, w_down)
    h = _out_call(ya, yb, g_mix_a, w_out_b, x2d, ln1_g, ln1_b)
    return _ffn_call(h, w_up_b, w_down_b, ln2_g, ln2_b)


def kernel(x, w_in, b_in, sinks, g_mix_a, g_mix_b, w_out, ln1_g, ln1_b, w_up, w_down, ln2_g, ln2_b):
    batch, seq, d_model = x.shape
    assert (batch, d_model) == (1, D_MODEL) and seq % A_TILE == 0
    h = x.reshape(seq, d_model)
    for i in range(w_in.shape[0]):
        h = _layer(h, w_in[i], b_in[i], sinks[i], g_mix_a[i], g_mix_b[i], w_out[i],
                   ln1_g[i], ln1_b[i], w_up[i], w_down[i], ln2_g[i], ln2_b[i])
    return h.reshape(batch, seq, d_model)
```

```python
import functools
import math

import jax
import jax.numpy as jnp
from jax import lax
from jax.experimental import pallas as pl
from jax.experimental.pallas import tpu as pltpu

F32 = jnp.float32
BF16 = jnp.bfloat16

D_MODEL = 2048
A_HEAD_DIM = 128
A_WIDTH = 1024
A_HEADS = 8
MID_DIL = 4
FAR_DIL = 16
B_HEAD_DIM = 64
B_WIDTH = 1024
B_HEADS = 16
B_KV_HEADS = 2
B_GROUP = 8
B_KV_WIDTH = 128
QKV_A_WIDTH = 3 * A_WIDTH
QKV_B_WIDTH = B_WIDTH + 2 * B_KV_WIDTH
IN_WIDTH = QKV_A_WIDTH + QKV_B_WIDTH
D_FF = 5632
BLOCK = 128
ROPE_THETA = 10000.0
ALPHA = 2.0 ** 0.25
LN_EPS = 1e-5
RMS_EPS = 1e-6
LOG2_E = math.log2(math.e)

LANES = 128
V7X_VMEM_BYTES = 64 * 1024 * 1024
V7X_VMEM_REQUEST_CAP = V7X_VMEM_BYTES - 2 * 1024 * 1024

PROJ_ROWS = 512
PROJ_CHUNK = 512
A_TILE = FAR_DIL * BLOCK
B_ROWS = 1024
OUT_ROWS = 512
OUT_SUB_ROWS = 256
FFN_ROWS = 1024
FFN_CHUNK = 512
FFN_NORM_ROWS = 256

A_MID_ROWS = A_TILE // MID_DIL
NEAR_SLAB = BLOCK // MID_DIL
A_STEP_HEADS = 2
GROUP = 4
GROUPS_AHEAD = 1
B_ITEM_GROUP = 2


def _vmem_limit(estimate_bytes, result_bytes=0):
    request = estimate_bytes + estimate_bytes // 4
    if result_bytes:
        request = max(request, V7X_VMEM_BYTES - result_bytes + result_bytes // 8)
    return int(min(V7X_VMEM_REQUEST_CAP, request))


def _params(semantics, vmem_estimate, result_bytes=0):
    return pltpu.CompilerParams(dimension_semantics=semantics,
                                vmem_limit_bytes=_vmem_limit(vmem_estimate, result_bytes))


def _rope_factors(seq, tile, head_dim):
    half = head_dim // 2
    inv_freq = ROPE_THETA ** (-jnp.arange(half, dtype=F32) / half)
    reps = LANES // head_dim

    def expand(pos):
        ang = pos.astype(F32)[:, None] * inv_freq[None, :]
        cos = jnp.cos(ang)
        sin = jnp.sin(ang)
        return (jnp.tile(jnp.concatenate([cos, cos], axis=-1), (1, reps)),
                jnp.tile(jnp.concatenate([sin, sin], axis=-1), (1, reps)))

    return expand(jnp.arange(tile)) + expand(jnp.arange(seq // tile) * tile)


def _proj_kernel(x_ref, w_ref, b_ref, oca_ref, osa_ref, tca_ref, tsa_ref,
                 ocb_ref, osb_ref, tcb_ref, tsb_ref, wout_ref, wup_ref,
                 o4_ref, o16_ref, ob_ref, wout_bf_ref, wup_bf_ref, nat_buf, mid_buf):
    rows = x_ref.shape[0]
    step = pl.program_id(0)
    wout_bf_ref[...] = wout_ref[...].astype(BF16)
    wup_bf_ref[...] = wup_ref[...].astype(BF16)
    xb = x_ref[...].astype(BF16)
    lane = lax.broadcasted_iota(jnp.int32, (rows, LANES), 1)
    first_half_a = lane < (A_HEAD_DIM // 2)
    first_half_b = (lane % B_HEAD_DIM) < (B_HEAD_DIM // 2)

    def tables(off_c_ref, off_s_ref, start_c_ref, start_s_ref, first_half):
        oc, os_ = off_c_ref[...], off_s_ref[...]
        tc, ts = start_c_ref[pl.ds(step, 1), :], start_s_ref[pl.ds(step, 1), :]
        sin = ts * oc + tc * os_
        return tc * oc - ts * os_, jnp.where(first_half, -sin, sin)

    ca, sa = tables(oca_ref, osa_ref, tca_ref, tsa_ref, first_half_a)
    cb, sb = tables(ocb_ref, osb_ref, tcb_ref, tsb_ref, first_half_b)
    scale_a = LOG2_E / math.sqrt(A_HEAD_DIM)
    scale_b = LOG2_E / math.sqrt(B_HEAD_DIM)
    mid_rows = rows // MID_DIL
    far_rows = rows // FAR_DIL

    def rope_a(y):
        return y * ca + pltpu.roll(y, A_HEAD_DIM // 2, 1) * sa

    def rope_b(y):
        partner = jnp.where(first_half_b,
                            pltpu.roll(y, LANES - B_HEAD_DIM // 2, 1),
                            pltpu.roll(y, B_HEAD_DIM // 2, 1))
        return y * cb + partner * sb

    starts = list(range(0, IN_WIDTH, PROJ_CHUNK))
    b_query = [c for c in starts if QKV_A_WIDTH <= c < QKV_A_WIDTH + B_WIDTH]
    for c0 in b_query + [c for c in starts if c not in b_query]:
        c1 = min(c0 + PROJ_CHUNK, IN_WIDTH)
        acc = jnp.dot(xb, w_ref[:, c0:c1], preferred_element_type=F32) + b_ref[:, c0:c1]
        for g, col in enumerate(range(c0, c1, LANES)):
            y = acc[:, col - c0:col - c0 + LANES]
            if col < A_WIDTH:
                y = rope_a(y) * scale_a
            elif col < 2 * A_WIDTH:
                y = rope_a(y)
            elif col < QKV_A_WIDTH:
                pass
            elif col < QKV_A_WIDTH + B_WIDTH:
                y = rope_b(y) * scale_b
            elif col < QKV_A_WIDTH + B_WIDTH + B_KV_WIDTH:
                y = rope_b(y)
            if col >= QKV_A_WIDTH:
                ob_ref[:, col - QKV_A_WIDTH:col - QKV_A_WIDTH + LANES] = y.astype(BF16)
                continue
            nat_buf[g] = y
            for c in range(MID_DIL):
                cls = nat_buf[g, pl.ds(c, mid_rows, stride=MID_DIL), :]
                o4_ref[c, :, col:col + LANES] = cls.astype(BF16)
                mid_buf[g, c * mid_rows:(c + 1) * mid_rows, :] = cls
            for c in range(MID_DIL):
                for c2 in range(FAR_DIL // MID_DIL):
                    cls = mid_buf[g, pl.ds(c * mid_rows + c2, far_rows, stride=MID_DIL), :]
                    o16_ref[c + MID_DIL * c2, :, col:col + LANES] = cls.astype(BF16)


def _proj_call(x2d, w_in, b_in, w_out, w_up):
    seq = x2d.shape[0]
    tm = PROJ_ROWS
    steps = seq // tm
    rope_a = _rope_factors(seq, tm, A_HEAD_DIM)
    rope_b = _rope_factors(seq, tm, B_HEAD_DIM)
    offset_spec = pl.BlockSpec((tm, LANES), lambda i: (0, 0))
    start_spec = pl.BlockSpec((steps, LANES), lambda i: (0, 0))
    rope_specs = [offset_spec, offset_spec, start_spec, start_spec]
    wout_rows = D_MODEL // steps
    groups = PROJ_CHUNK // LANES
    vmem = (D_MODEL * IN_WIDTH * 2
            + 2 * tm * D_MODEL * 4
            + 2 * tm * (2 * QKV_A_WIDTH + QKV_B_WIDTH) * 2
            + 2 * 4 * (tm + steps) * LANES * 4
            + 2 * wout_rows * (D_MODEL + 2 * D_FF) * 6
            + 2 * groups * tm * LANES * 4
            + tm * D_MODEL * 2 + 2 * tm * PROJ_CHUNK * 4)
    return pl.pallas_call(
        _proj_kernel,
        grid=(steps,),
        in_specs=[
            pl.BlockSpec((tm, D_MODEL), lambda i: (i, 0)),
            pl.BlockSpec((D_MODEL, IN_WIDTH), lambda i: (0, 0), pipeline_mode=pl.Buffered(1)),
            pl.BlockSpec((1, IN_WIDTH), lambda i: (0, 0)),
            *rope_specs, *rope_specs,
            pl.BlockSpec((wout_rows, D_MODEL), lambda i: (i, 0)),
            pl.BlockSpec((wout_rows, 2 * D_FF), lambda i: (i, 0)),
        ],
        out_specs=[
            pl.BlockSpec((MID_DIL, tm // MID_DIL, QKV_A_WIDTH), lambda i: (0, i, 0)),
            pl.BlockSpec((FAR_DIL, tm // FAR_DIL, QKV_A_WIDTH), lambda i: (0, i, 0)),
            pl.BlockSpec((tm, QKV_B_WIDTH), lambda i: (i, 0)),
            pl.BlockSpec((wout_rows, D_MODEL), lambda i: (i, 0)),
            pl.BlockSpec((wout_rows, 2 * D_FF), lambda i: (i, 0)),
        ],
        out_shape=[
            jax.ShapeDtypeStruct((MID_DIL, seq // MID_DIL, QKV_A_WIDTH), BF16),
            jax.ShapeDtypeStruct((FAR_DIL, seq // FAR_DIL, QKV_A_WIDTH), BF16),
            jax.ShapeDtypeStruct((seq, QKV_B_WIDTH), BF16),
            jax.ShapeDtypeStruct((D_MODEL, D_MODEL), BF16),
            jax.ShapeDtypeStruct((D_MODEL, 2 * D_FF), BF16),
        ],
        scratch_shapes=[pltpu.VMEM((groups, tm, LANES), F32), pltpu.VMEM((groups, tm, LANES), F32)],
        compiler_params=_params(("arbitrary",), vmem),
        name="proj_rope",
    )(x2d, w_in.astype(BF16), b_in.reshape(1, IN_WIDTH), *rope_a, *rope_b, w_out, w_up)


def _band_masks(lo, is_first_tile):
    qi = lax.broadcasted_iota(jnp.int32, (BLOCK, 2 * BLOCK), 0)
    kj = lax.broadcasted_iota(jnp.int32, (BLOCK, 2 * BLOCK), 1)
    band = (kj >= qi + lo) & (kj <= qi + BLOCK)
    first_lower = jnp.where(is_first_tile, BLOCK, 0)
    return band, band & (kj >= first_lower)


def _near_masks(is_first_tile):
    iq = lax.broadcasted_iota(jnp.int32, (BLOCK, 2 * BLOCK), 0)
    ik = lax.broadcasted_iota(jnp.int32, (BLOCK, 2 * BLOCK), 1)
    cq, mq = iq // NEAR_SLAB, iq % NEAR_SLAB
    ck, mk = ik // (2 * NEAR_SLAB), ik % (2 * NEAR_SLAB)
    dist = MID_DIL * (mq + NEAR_SLAB - mk) + (cq - ck)
    valid = (dist >= 0) & (dist <= BLOCK)
    first_lower = jnp.where(is_first_tile, NEAR_SLAB, 0)
    return valid, valid & (mk >= first_lower)


def _rms_scale(sum_sq, width):
    return lax.rsqrt(sum_sq * (1.0 / width) + RMS_EPS)


def _qk(q, k):
    return lax.dot_general(q, k, (((1,), (1,)), ((), ())), preferred_element_type=F32)


def _attn_a_kernel(q4_ref, k4_ref, k4h_ref, v4_ref, v4h_ref,
                   q16_ref, k16_ref, k16h_ref, v16_ref, v16h_ref, y_ref,
                   k4buf, v4buf, k16buf, v16buf, *state_refs):
    first = pl.program_id(0) == 0
    k4buf[:, 0:BLOCK, :] = k4h_ref[...]
    k4buf[:, BLOCK:, :] = k4_ref[...]
    v4buf[:, 0:BLOCK, :] = v4h_ref[...]
    v4buf[:, BLOCK:, :] = v4_ref[...]
    k16buf[:, 0:BLOCK, :] = k16h_ref[...]
    k16buf[:, BLOCK:, :] = k16_ref[...]
    v16buf[:, 0:BLOCK, :] = v16h_ref[...]
    v16buf[:, BLOCK:, :] = v16_ref[...]

    near, near_first = _near_masks(first)
    band, band_first = _band_masks(0, first)
    full = (BLOCK, LANES)
    ones = jnp.ones((2 * BLOCK, LANES), BF16)

    def head_items(hd):
        hs = slice(hd * A_HEAD_DIM, (hd + 1) * A_HEAD_DIM)
        acc_f, m_f, l_f, acc_m, m_m, l_m, y_s = state_refs[7 * hd:7 * hd + 7]

        def far_item(c, c2):
            r = c + MID_DIL * c2
            rows = pl.ds(c * A_MID_ROWS + c2, BLOCK, stride=MID_DIL)

            def store(m, l, acc):
                acc_f[rows, :] = acc
                m_f[rows, :] = m
                l_f[rows, :] = l

            return (lambda: q16_ref[r, :, hs], lambda: k16buf[r, :, hs], lambda: v16buf[r, :, hs],
                    band_first, None, store)

        def mid_item(c, j):
            rows = slice(c * A_MID_ROWS + j * BLOCK, c * A_MID_ROWS + (j + 1) * BLOCK)

            def store(m, l, acc):
                acc_m[rows, :] = acc
                m_m[rows, :] = m
                l_m[rows, :] = l

            return (lambda: q4_ref[c, j * BLOCK:(j + 1) * BLOCK, hs],
                    lambda: k4buf[c, j * BLOCK:(j + 2) * BLOCK, hs],
                    lambda: v4buf[c, j * BLOCK:(j + 2) * BLOCK, hs],
                    band_first if j == 0 else band,
                    lambda: (m_f[rows, :], l_f[rows, :], acc_f[rows, :]), store)

        def near_item(j):
            r0 = j * NEAR_SLAB
            h0 = BLOCK + r0 - NEAR_SLAB

            def gather(ref, start, size):
                return jnp.concatenate([ref[c, start:start + size, hs] for c in range(MID_DIL)],
                                       axis=0)

            def state(buf):
                return jnp.concatenate([buf[c * A_MID_ROWS + r0:c * A_MID_ROWS + r0 + NEAR_SLAB, :]
                                        for c in range(MID_DIL)], axis=0)

            def store(m, l, acc):
                out = acc * (1.0 / l)
                for c in range(MID_DIL):
                    y_s[pl.ds(j * BLOCK + c, NEAR_SLAB, stride=MID_DIL), :] = (
                        out[c * NEAR_SLAB:(c + 1) * NEAR_SLAB])

            return (lambda: gather(q4_ref, r0, NEAR_SLAB), lambda: gather(k4buf, h0, 2 * NEAR_SLAB),
                    lambda: gather(v4buf, h0, 2 * NEAR_SLAB), near_first if j == 0 else near,
                    lambda: (state(m_m), state(l_m), state(acc_m)), store)

        items = [far_item(c, c2) for c in range(MID_DIL) for c2 in range(FAR_DIL // MID_DIL)]
        items += [mid_item(c, j) for c in range(MID_DIL) for j in range(A_MID_ROWS // BLOCK)]
        items += [near_item(j) for j in range(A_TILE // BLOCK)]
        return items

    per_head = [head_items(hd) for hd in range(A_STEP_HEADS)]
    work = [item for g in range(0, len(per_head[0]), GROUP)
            for items in per_head for item in items[g:g + GROUP]]
    groups = [work[g:g + GROUP] for g in range(0, len(work), GROUP)]

    def score_group(items):
        return [_qk(q(), k()) for q, k, _, _, _, _ in items]

    def finish_group(items, scores):
        staged = []
        for (_, _, _, valid, old, _), s in zip(items, scores):
            s = jnp.where(valid, s, -jnp.inf)
            m = jnp.max(s, axis=1, keepdims=True)
            if old is None:
                staged.append((jnp.broadcast_to(m, full), jnp.exp2(s - m).astype(BF16), None))
                continue
            m_old, l_old, acc_old = old()
            m = jnp.maximum(m_old, m)
            p = jnp.concatenate([jnp.exp2(s[:, :LANES] - m), jnp.exp2(s[:, LANES:] - m)], axis=1)
            staged.append((m, p.astype(BF16), (jnp.exp2(m_old - m), l_old, acc_old)))
        for (_, _, v, _, _, store), (m, p, old) in zip(items, staged):
            pv = jnp.dot(p, jnp.concatenate([v(), ones], axis=1), preferred_element_type=F32)
            acc, l = pv[:, :LANES], pv[:, LANES:]
            if old is not None:
                rescale, l_old, acc_old = old
                acc = rescale * acc_old + acc
                l = rescale * l_old + l
            store(m, l, acc)

    pending = [score_group(items) for items in groups[:GROUPS_AHEAD]]
    for g, items in enumerate(groups):
        if g + GROUPS_AHEAD < len(groups):
            pending.append(score_group(groups[g + GROUPS_AHEAD]))
        finish_group(items, pending.pop(0))

    for hd in range(A_STEP_HEADS):
        y_ref[:, hd * A_HEAD_DIM:(hd + 1) * A_HEAD_DIM] = state_refs[7 * hd + 6][...].astype(BF16)


def _attn_a_call(qkv4, qkv16):
    seq = qkv4.shape[0] * qkv4.shape[1]
    tiles = seq // A_TILE
    mid_blocks = A_MID_ROWS // BLOCK

    width = A_STEP_HEADS * A_HEAD_DIM
    col_steps = A_HEADS // A_STEP_HEADS

    def mid_tile(which):
        return pl.BlockSpec((MID_DIL, A_MID_ROWS, width),
                            lambda t, h: (0, t, which * col_steps + h))

    def mid_halo(which):
        return pl.BlockSpec((MID_DIL, BLOCK, width),
                            lambda t, h: (0, jnp.maximum(t * mid_blocks - 1, 0), which * col_steps + h))

    def far_tile(which):
        return pl.BlockSpec((FAR_DIL, BLOCK, width), lambda t, h: (0, t, which * col_steps + h))

    def far_halo(which):
        return pl.BlockSpec((FAR_DIL, BLOCK, width),
                            lambda t, h: (0, jnp.maximum(t - 1, 0), which * col_steps + h))

    slab = A_TILE * width
    vmem = (2 * (3 * slab + 2 * slab // 4) * 2
            + 2 * 5 * slab * 2
            + 2 * slab * 2
            + 2 * (slab + slab // 4) * 2 + 2 * 2 * slab * 2
            + 7 * slab * 4)
    return pl.pallas_call(
        _attn_a_kernel,
        grid=(tiles, col_steps),
        in_specs=[mid_tile(0), mid_tile(1), mid_halo(1), mid_tile(2), mid_halo(2),
                  far_tile(0), far_tile(1), far_halo(1), far_tile(2), far_halo(2)],
        out_specs=pl.BlockSpec((A_TILE, width), lambda t, h: (t, h)),
        out_shape=jax.ShapeDtypeStruct((seq, A_WIDTH), BF16),
        scratch_shapes=[
            pltpu.VMEM((MID_DIL, A_MID_ROWS + BLOCK, width), BF16),
            pltpu.VMEM((MID_DIL, A_MID_ROWS + BLOCK, width), BF16),
            pltpu.VMEM((FAR_DIL, 2 * BLOCK, width), BF16),
            pltpu.VMEM((FAR_DIL, 2 * BLOCK, width), BF16),
        ] + [pltpu.VMEM((A_TILE, LANES), F32)] * (7 * A_STEP_HEADS),
        compiler_params=_params(("arbitrary", "arbitrary"), vmem, result_bytes=seq * A_WIDTH * 2),
        name="attn_a",
    )(qkv4, qkv4, qkv4, qkv4, qkv4, qkv16, qkv16, qkv16, qkv16, qkv16)


def _attn_b_kernel(sinks_ref, q_ref, km_ref, kh_ref, vm_ref, vh_ref, gain_ref, wdn_ref,
                   y_ref, wdn_bf_ref, klo, khi, vlo, vhi, obuf, *, rows):
    wdn_bf_ref[...] = wdn_ref[...].astype(BF16)
    lane = lax.broadcasted_iota(jnp.int32, (rows + BLOCK, LANES), 1)
    low = lane < B_HEAD_DIM

    def fill(dst_lo, dst_hi, halo_ref, main_ref):
        full = jnp.concatenate([halo_ref[...], main_ref[...]], axis=0).astype(F32)
        swapped = pltpu.roll(full, B_HEAD_DIM, 1)
        zero = jnp.zeros_like(full)
        for c in range(B_KV_HEADS):
            own_low, own_high = (full, swapped) if c == 0 else (swapped, full)
            dst_lo[c] = jnp.where(low, own_low, zero).astype(BF16)
            dst_hi[c] = jnp.where(low, zero, own_high).astype(BF16)

    fill(klo, khi, kh_ref, km_ref)
    fill(vlo, vhi, vh_ref, vm_ref)

    band, band_first = _band_masks(1, pl.program_id(0) == 0)

    items = [(j, pair) for j in range(rows // BLOCK) for pair in range(B_HEADS // 2)]
    groups = [items[g:g + B_ITEM_GROUP] for g in range(0, len(items), B_ITEM_GROUP)]

    col0 = lax.broadcasted_iota(jnp.int32, (1, LANES), 1) == 0
    neg_inf_row = jnp.full((1, LANES), -jnp.inf, F32)

    def score_group(group):
        out = []
        for j, pair in group:
            c = (2 * pair) // B_GROUP
            r0 = j * BLOCK
            q = q_ref[r0:r0 + BLOCK, pair * LANES:(pair + 1) * LANES]
            valid = band_first if j == 0 else band
            pair_scores = []
            for odd, k_sel in enumerate((klo, khi)):
                s = _qk(q, k_sel[c, r0:r0 + 2 * BLOCK, :])
                sink_row = jnp.where(col0, sinks_ref[2 * pair + odd] * LOG2_E, neg_inf_row)
                pair_scores.append(jnp.concatenate(
                    [jnp.where(valid[:, :LANES], s[:, :LANES], sink_row),
                     jnp.where(valid[:, LANES:], s[:, LANES:], -jnp.inf)], axis=1))
            out.append(pair_scores)
        return out

    def finish_group(group, scores):
        staged = []
        for (j, pair), pair_scores in zip(group, scores):
            for odd, s in enumerate(pair_scores):
                m = jnp.max(s, axis=1, keepdims=True)
                p = jnp.exp2(s - m)
                l = jnp.sum(p, axis=1, keepdims=True)
                p = jnp.concatenate([jnp.where(col0, 0.0, p[:, :LANES]), p[:, LANES:]], axis=1)
                staged.append((p.astype(BF16), 1.0 / l))
        for i, (j, pair) in enumerate(group):
            c = (2 * pair) // B_GROUP
            r0 = j * BLOCK
            o = jnp.zeros((BLOCK, LANES), F32)
            for odd, v_sel in enumerate((vlo, vhi)):
                p, weight = staged[2 * i + odd]
                acc = jnp.dot(p, v_sel[c, r0:r0 + 2 * BLOCK, :], preferred_element_type=F32)
                o = o + acc * weight
            obuf[r0:r0 + BLOCK, pair * LANES:(pair + 1) * LANES] = o

    scores = score_group(groups[0])
    for g, group in enumerate(groups):
        upcoming = score_group(groups[g + 1]) if g + 1 < len(groups) else None
        finish_group(group, scores)
        scores = upcoming

    for j in range(rows // BLOCK):
        o = obuf[j * BLOCK:(j + 1) * BLOCK, :]
        scale = _rms_scale(jnp.sum(o * o, axis=1, keepdims=True), B_WIDTH)
        y_ref[j * BLOCK:(j + 1) * BLOCK, :] = (o * scale * gain_ref[...]).astype(BF16)


def _attn_b_call(qkv_b, sinks, gain, w_down):
    seq = qkv_b.shape[0]
    rows = B_ROWS
    blocks_per_tile = rows // BLOCK
    down_rows = D_FF // (seq // rows)
    k_col = B_WIDTH // B_KV_WIDTH
    v_col = k_col + 1

    def kv_tile(col):
        return pl.BlockSpec((rows, B_KV_WIDTH), lambda t: (t, col))

    def kv_halo(col):
        return pl.BlockSpec((BLOCK, B_KV_WIDTH),
                            lambda t: (jnp.maximum(t * blocks_per_tile - 1, 0), col))

    kv_scratch = pltpu.VMEM((B_KV_HEADS, rows + BLOCK, LANES), BF16)
    vmem = (2 * 2 * rows * B_WIDTH * 2 + 4 * 2 * (rows + BLOCK) * LANES * 2
            + 4 * B_KV_HEADS * (rows + BLOCK) * LANES * 2 + rows * B_WIDTH * 4
            + 4 * (rows + BLOCK) * LANES * 4
            + 2 * down_rows * D_MODEL * 6)
    return pl.pallas_call(
        functools.partial(_attn_b_kernel, rows=rows),
        grid=(seq // rows,),
        in_specs=[
            pl.BlockSpec(memory_space=pltpu.SMEM),
            pl.BlockSpec((rows, B_WIDTH), lambda t: (t, 0)),
            kv_tile(k_col), kv_halo(k_col), kv_tile(v_col), kv_halo(v_col),
            pl.BlockSpec((1, B_WIDTH), lambda t: (0, 0)),
            pl.BlockSpec((down_rows, D_MODEL), lambda t: (t, 0)),
        ],
        out_specs=[
            pl.BlockSpec((rows, B_WIDTH), lambda t: (t, 0)),
            pl.BlockSpec((down_rows, D_MODEL), lambda t: (t, 0)),
        ],
        out_shape=[
            jax.ShapeDtypeStruct((seq, B_WIDTH), BF16),
            jax.ShapeDtypeStruct((D_FF, D_MODEL), BF16),
        ],
        scratch_shapes=[kv_scratch, kv_scratch, kv_scratch, kv_scratch,
                        pltpu.VMEM((rows, B_WIDTH), F32)],
        compiler_params=_params(("arbitrary",), vmem, result_bytes=seq * B_WIDTH * 2),
        name="attn_b",
    )(sinks, qkv_b, qkv_b, qkv_b, qkv_b, qkv_b, gain.reshape(1, B_WIDTH), w_down)


def _layer_norm_rows(z, g, b):
    mu = jnp.mean(z, axis=-1, keepdims=True)
    zc = z - mu
    var = jnp.mean(zc * zc, axis=-1, keepdims=True)
    return zc * lax.rsqrt(var + LN_EPS) * g + b


def _out_kernel(ya_ref, yb_ref, ga_ref, w_ref, x_ref, g_ref, b_ref, h_ref):
    blocks = [slice(r, r + OUT_SUB_ROWS) for r in range(0, ya_ref.shape[0], OUT_SUB_ROWS)]
    mixes = []
    for rows in blocks:
        ya = ya_ref[rows, :].astype(F32)
        scale = _rms_scale(jnp.sum(ya * ya, axis=1, keepdims=True), A_WIDTH)
        ya = (ya * scale * ga_ref[...]).astype(BF16)
        mix = jnp.dot(ya, w_ref[0:A_WIDTH, :], preferred_element_type=F32)
        mixes.append(mix + jnp.dot(yb_ref[rows, :], w_ref[A_WIDTH:, :], preferred_element_type=F32))
    for rows, mix in zip(blocks, mixes):
        h_ref[rows, :] = _layer_norm_rows(ALPHA * x_ref[rows, :] + mix, g_ref[...], b_ref[...])


def _out_call(ya, yb, gain_a, w_out_bf, x2d, ln_g, ln_b):
    seq = x2d.shape[0]
    tm = OUT_ROWS
    row_spec = pl.BlockSpec((1, D_MODEL), lambda i: (0, 0))
    vmem = (D_MODEL * D_MODEL * 2 + 2 * 2 * tm * A_WIDTH * 2
            + 2 * 2 * tm * D_MODEL * 4 + 2 * tm * D_MODEL * 4)
    return pl.pallas_call(
        _out_kernel,
        grid=(seq // tm,),
        in_specs=[
            pl.BlockSpec((tm, A_WIDTH), lambda i: (i, 0)),
            pl.BlockSpec((tm, B_WIDTH), lambda i: (i, 0)),
            pl.BlockSpec((1, A_WIDTH), lambda i: (0, 0)),
            pl.BlockSpec((D_MODEL, D_MODEL), lambda i: (0, 0), pipeline_mode=pl.Buffered(1)),
            pl.BlockSpec((tm, D_MODEL), lambda i: (i, 0)),
            row_spec, row_spec,
        ],
        out_specs=pl.BlockSpec((tm, D_MODEL), lambda i: (i, 0)),
        out_shape=jax.ShapeDtypeStruct((seq, D_MODEL), F32),
        compiler_params=_params(("arbitrary",), vmem),
        name="out_proj_ln",
    )(ya, yb, gain_a.reshape(1, A_WIDTH), w_out_bf, x2d,
      ln_g.reshape(1, D_MODEL), ln_b.reshape(1, D_MODEL))


def _ffn_kernel(h_ref, wg_ref, wu_ref, wd_ref, g_ref, b_ref, o_ref, hb_ref):
    j = pl.program_id(1)
    last = pl.num_programs(1) - 1
    half = wg_ref.shape[1] // 2
    halves = [slice(0, half), slice(half, 2 * half)]

    def activations(hb):
        gates = [jnp.dot(hb, wg_ref[:, s], preferred_element_type=F32) for s in halves]
        ups = [jnp.dot(hb, wu_ref[:, s], preferred_element_type=F32) for s in halves]
        return [(g * jax.nn.sigmoid(g) * u).astype(BF16) for g, u in zip(gates, ups)]

    def down(acts, rows=slice(None)):
        return (jnp.dot(acts[0][rows], wd_ref[halves[0], :], preferred_element_type=F32)
                + jnp.dot(acts[1][rows], wd_ref[halves[1], :], preferred_element_type=F32))

    @pl.when(j == 0)
    def _():
        h = h_ref[...]
        hb = h.astype(BF16)
        hb_ref[...] = hb
        o_ref[...] = ALPHA * h + down(activations(hb))

    @pl.when((j > 0) & (j < last))
    def _():
        o_ref[...] += down(activations(hb_ref[...]))

    @pl.when(j == last)
    def _():
        acts = activations(hb_ref[...])
        for r in range(0, o_ref.shape[0], FFN_NORM_ROWS):
            rows = slice(r, r + FFN_NORM_ROWS)
            o_ref[rows, :] = _layer_norm_rows(o_ref[rows, :] + down(acts, rows),
                                              g_ref[...], b_ref[...])


def _ffn_call(h, w_up_b, w_down_b, ln_g, ln_b):
    seq = h.shape[0]
    tm, tf = FFN_ROWS, FFN_CHUNK
    n_chunks = D_FF // tf
    assert n_chunks >= 2
    row_spec = pl.BlockSpec((1, D_MODEL), lambda i, j: (0, 0))
    vmem = (2 * tm * D_MODEL * 4 + tm * D_MODEL * 2 + 2 * tm * D_MODEL * 4
            + 2 * 3 * D_MODEL * tf * 2 + 3 * tm * tf * 4)
    return pl.pallas_call(
        _ffn_kernel,
        grid=(seq // tm, n_chunks),
        in_specs=[
            pl.BlockSpec((tm, D_MODEL), lambda i, j: (i, 0)),
            pl.BlockSpec((D_MODEL, tf), lambda i, j: (0, j)),
            pl.BlockSpec((D_MODEL, tf), lambda i, j: (0, n_chunks + j)),
            pl.BlockSpec((tf, D_MODEL), lambda i, j: (j, 0)),
            row_spec, row_spec,
        ],
        out_specs=pl.BlockSpec((tm, D_MODEL), lambda i, j: (i, 0)),
        out_shape=jax.ShapeDtypeStruct((seq, D_MODEL), F32),
        scratch_shapes=[pltpu.VMEM((tm, D_MODEL), BF16)],
        compiler_params=_params(("arbitrary", "arbitrary"), vmem),
        name="swiglu_ln",
    )(h, w_up_b, w_up_b, w_down_b, ln_g.reshape(1, D_MODEL), ln_b.reshape(1, D_MODEL))


def _layer(x2d, w_in, b_in, sinks, g_mix_a, g_mix_b, w_out, ln1_g, ln1_b, w_up, w_down, ln2_g, ln2_b):
    qkv4, qkv16, qkv_b, w_out_b, w_up_b = _proj_call(x2d, w_in, b_in, w_out, w_up)
    ya = _attn_a_call(qkv4, qkv16)
    yb, w_down_b = _attn_b_call(qkv_b, sinks, g_mix_b, w_down)
    h = _out_call(ya, yb, g_mix_a, w_out_b, x2d, ln1_g, ln1_b)
    return _ffn_call(h, w_up_b, w_down_b, ln2_g, ln2_b)


def kernel(x, w_in, b_in, sinks, g_mix_a, g_mix_b, w_out, ln1_g, ln1_b, w_up, w_down, ln2_g, ln2_b):
    batch, seq, d_model = x.shape
    assert (batch, d_model) == (1, D_MODEL) and seq % A_TILE == 0
    h = x.reshape(seq, d_model)
    for i in range(w_in.shape[0]):
        h = _layer(h, w_in[i], b_in[i], sinks[i], g_mix_a[i], g_mix_b[i], w_out[i],
                   ln1_g[i], ln1_b[i], w_up[i], w_down[i], ln2_g[i], ln2_b[i])
    return h.reshape(batch, seq, d_model)
```
